```python
import math
import jax, jax.numpy as jnp
from jax import lax
import numpy as np

D_MODEL = 2048
BATCH = 4
SEQ = 8192
DEPTH = 4
DEC_BATCH = 16
DEC_SEQ = 64
PAST_LEN = 2048

CHUNK = 64
N_MIXERS = 2
N_RET_LAYERS = (DEPTH + 1) // 2
N_DIFF_LAYERS = DEPTH // 2
RET_HEADS = 8
RET_DK = D_MODEL // RET_HEADS
RET_DV = D_MODEL // RET_HEADS
DIFF_HEADS = 8
DIFF_DH = D_MODEL // (2 * DIFF_HEADS)
DIFF_DV = 2 * DIFF_DH
D_FF = ((-(-8 * D_MODEL // 3) + 255) // 256) * 256
N_BUCKETS = 32
MAX_DISTANCE = 128
Q_BLOCK = 128
NORM_EPS = 1e-6
ROPE_BASE = 10000.0

kernel_name = "retnet_diffattn_streaming_hybrid"

F32 = jnp.float32


def rms_norm(x, g):
    x32 = x.astype(F32)
    y = x32 * lax.rsqrt(jnp.mean(x32 * x32, axis=-1, keepdims=True) + NORM_EPS)
    return (y * g.astype(F32)).astype(x.dtype)


def rotary(x, pos):
    half = x.shape[-1] // 2
    inv = ROPE_BASE ** (-jnp.arange(half, dtype=F32) / half)
    ang = pos.astype(F32)[:, None] * inv[None, :]
    cos = jnp.cos(ang)[None, :, None, :]
    sin = jnp.sin(ang)[None, :, None, :]
    x32 = x.astype(F32)
    x1, x2 = x32[..., :half], x32[..., half:]
    return jnp.concatenate([x1 * cos - x2 * sin, x1 * sin + x2 * cos], axis=-1).astype(x.dtype)


def retention_scan(q, k, v, state):
    B, T, H, _ = q.shape
    dv = v.shape[-1]
    L = min(T, CHUNK)
    n = T // L
    lg = jnp.log1p(-jnp.exp2(-5.0 - jnp.arange(H, dtype=F32)))
    idx = jnp.arange(L, dtype=F32)
    rel = idx[:, None] - idx[None, :]
    decay = jnp.where(rel[None] >= 0, jnp.exp(rel[None] * lg[:, None, None]), 0.0)
    q_dec = jnp.exp((idx + 1.0)[None, :] * lg[:, None])[None, :, :, None]
    k_dec = jnp.exp((L - 1.0 - idx)[None, :] * lg[:, None])[None, :, :, None]
    chunk_dec = jnp.exp(L * lg)[None, :, None, None]

    def to_chunks(a):
        return a.astype(F32).reshape(B, n, L, H, a.shape[-1]).transpose(1, 0, 3, 2, 4)

    def step(S, qkv):
        qc, kc, vc = qkv
        scores = jnp.einsum('bhld,bhmd->bhlm', qc, kc) * decay[None]
        o = (jnp.einsum('bhlm,bhme->bhle', scores, vc)
             + jnp.einsum('bhld,bhde->bhle', qc, S) * q_dec)
        S = S * chunk_dec + jnp.einsum('bhld,bhle->bhde', kc * k_dec, vc)
        return S, o

    S, o = lax.scan(step, state.astype(F32), (to_chunks(q), to_chunks(k), to_chunks(v)))
    o = o.transpose(1, 0, 3, 2, 4).reshape(B, T, H, dv)
    return o.astype(v.dtype), S.astype(state.dtype)


def retention_mixer(h, pos, state, w_in, w_out, gn_g):
    B, T, _ = h.shape
    q, k, v, g = jnp.split(h @ w_in, 4, axis=-1)
    q = rotary(q.reshape(B, T, RET_HEADS, RET_DK), pos)
    k = rotary(k.reshape(B, T, RET_HEADS, RET_DK), pos) * (RET_DK ** -0.5)
    v = v.reshape(B, T, RET_HEADS, RET_DV)
    o, new_state = retention_scan(q, k, v, state)
    o = rms_norm(o, gn_g.reshape(RET_HEADS, RET_DV)).reshape(B, T, D_MODEL)
    return (jax.nn.silu(g) * o) @ w_out, new_state


def t5_bucket(rel):
    nb = N_BUCKETS // 2
    max_exact = nb // 2
    bucket = jnp.where(rel > 0, nb, 0)
    n = jnp.abs(rel)
    large = max_exact + (jnp.log(jnp.maximum(n, 1).astype(F32) / max_exact)
                         / math.log(MAX_DISTANCE / max_exact) * (nb - max_exact)).astype(jnp.int32)
    large = jnp.minimum(large, nb - 1)
    return bucket + jnp.where(n < max_exact, n, large)


def diff_block(qb, k, v, q_pos, k_pos, lam, rel_bias):
    logits = jnp.einsum('bqhjd,bkhjd->bhjqk', qb, k, preferred_element_type=F32) * (DIFF_DH ** -0.5)
    bias = rel_bias.astype(F32)[t5_bucket(k_pos[None, :] - q_pos[:, None])]
    bias = jnp.transpose(bias, (2, 0, 1))[None, :, None]
    visible = (k_pos[None, :] // CHUNK) <= (q_pos[:, None] // CHUNK)
    logits = jnp.where(visible, logits + bias, -jnp.inf)
    p = jax.nn.softmax(logits, axis=-1)
    attn = p[:, :, 0] - lam * p[:, :, 1]
    return jnp.einsum('bhqk,bkhe->bqhe', attn.astype(v.dtype), v)


def diff_attention(q, k, v, q_pos, k_pos, lam, rel_bias):
    B, Tq = q.shape[:2]
    blk = min(Tq, Q_BLOCK)
    nb = Tq // blk
    qb = q.reshape(B, nb, blk, DIFF_HEADS, 2, DIFF_DH).swapaxes(0, 1)
    pb = q_pos.reshape(nb, blk)
    out = lax.map(lambda a: diff_block(a[0], k, v, a[1], k_pos, lam, rel_bias), (qb, pb))
    return out.swapaxes(0, 1).reshape(B, Tq, DIFF_HEADS, DIFF_DV)


def diff_mixer(h, pos, k_past, v_past, past_pos, w_in, w_out, lam_vecs, subln_g, lambda_init, rel_bias):
    B, T, _ = h.shape
    q, k, v = jnp.split(h @ w_in, 3, axis=-1)
    q = q.reshape(B, T, DIFF_HEADS, 2, DIFF_DH)
    k_new = k.reshape(B, T, DIFF_HEADS, 2 * DIFF_DH)
    v_new = v.reshape(B, T, DIFF_HEADS, DIFF_DV)
    if k_past is None:
        k_all, v_all, k_pos = k_new, v_new, pos
    else:
        k_all = jnp.concatenate([k_past.astype(k_new.dtype), k_new], axis=1)
        v_all = jnp.concatenate([v_past.astype(v_new.dtype), v_new], axis=1)
        k_pos = jnp.concatenate([past_pos, pos])
    lv = lam_vecs.astype(F32)
    lam = jnp.exp(jnp.sum(lv[0] * lv[1])) - jnp.exp(jnp.sum(lv[2] * lv[3])) + lambda_init
    o = diff_attention(q, k_all.reshape(B, -1, DIFF_HEADS, 2, DIFF_DH), v_all, pos, k_pos, lam, rel_bias)
    o = rms_norm(o, subln_g) * (1.0 - lambda_init)
    return o.reshape(B, T, D_MODEL) @ w_out, k_new, v_new


def swiglu(h, w_gu, w_down):
    gate, up = jnp.split(h @ w_gu, 2, axis=-1)
    return (jax.nn.silu(gate) * up) @ w_down


def setup_inputs(seed: int = 0) -> dict:
    key = jax.random.key(seed)
    ks = jax.random.split(key, 16)

    def nrm(k, shape, s):
        return jax.random.normal(k, shape, F32) * s

    D = D_MODEL
    return {
        "x_prompt": nrm(ks[0], (BATCH, SEQ, D), 1.0),
        "x_sample": nrm(ks[1], (DEC_BATCH, DEC_SEQ, D), 1.0),
        "state_ret": nrm(ks[2], (N_RET_LAYERS, DEC_BATCH, RET_HEADS, RET_DK, RET_DV), 0.1),
        "cache_k": nrm(ks[3], (N_DIFF_LAYERS, DEC_BATCH, PAST_LEN, DIFF_HEADS, 2 * DIFF_DH), 1.0),
        "cache_v": nrm(ks[4], (N_DIFF_LAYERS, DEC_BATCH, PAST_LEN, DIFF_HEADS, DIFF_DV), 1.0),
        "norm_g": 1.0 + nrm(ks[5], (DEPTH, 4, D), 0.05),
        "ret_w_in": nrm(ks[6], (N_RET_LAYERS, D, 4 * D), D ** -0.5),
        "ret_w_out": nrm(ks[7], (N_RET_LAYERS, D, D), D ** -0.5),
        "ret_gn_g": 1.0 + nrm(ks[8], (N_RET_LAYERS, D), 0.05),
        "diff_w_in": nrm(ks[9], (N_DIFF_LAYERS, D, 3 * D), D ** -0.5),
        "diff_w_out": nrm(ks[10], (N_DIFF_LAYERS, D, D), D ** -0.5),
        "diff_lambda": nrm(ks[11], (N_DIFF_LAYERS, 4, DIFF_DH), 0.1),
        "diff_subln_g": 1.0 + nrm(ks[12], (N_DIFF_LAYERS, DIFF_DV), 0.05),
        "rel_bias": nrm(ks[13], (N_BUCKETS, DIFF_HEADS), 0.3),
        "ffn_w_gu": nrm(ks[14], (DEPTH, D, 2 * D_FF), D ** -0.5),
        "ffn_w_down": nrm(ks[15], (DEPTH, D_FF, D), D_FF ** -0.5),
    }


def reference(x_prompt, x_sample, state_ret, cache_k, cache_v, norm_g, ret_w_in, ret_w_out, ret_gn_g,
              diff_w_in, diff_w_out, diff_lambda, diff_subln_g, rel_bias, ffn_w_gu, ffn_w_down):
    n_p = x_prompt.shape[1]
    n_s = x_sample.shape[1]
    past = cache_k.shape[2]
    pos_p = jnp.arange(n_p, dtype=jnp.int32)
    pos_s = past + jnp.arange(n_s, dtype=jnp.int32)
    past_pos = jnp.arange(past, dtype=jnp.int32)

    xp, xs = x_prompt, x_sample
    ret_p, ret_s, kp, vp, ksl, vsl = [], [], [], [], [], []
    for i in range(DEPTH):
        g = norm_g[i]
        hp = rms_norm(xp, g[0])
        hs = rms_norm(xs, g[0])
        j = i // N_MIXERS
        if i % N_MIXERS == 0:
            zero = jnp.zeros((xp.shape[0], RET_HEADS, RET_DK, RET_DV), xp.dtype)
            mp, sp = retention_mixer(hp, pos_p, zero, ret_w_in[j], ret_w_out[j], ret_gn_g[j])
            ms, ss = retention_mixer(hs, pos_s, state_ret[j], ret_w_in[j], ret_w_out[j], ret_gn_g[j])
            ret_p.append(sp)
            ret_s.append(ss)
        else:
            lambda_init = 0.8 - 0.6 * math.exp(-0.3 * i)
            mp, k1, v1 = diff_mixer(hp, pos_p, None, None, None, diff_w_in[j], diff_w_out[j],
                                    diff_lambda[j], diff_subln_g[j], lambda_init, rel_bias)
            ms, k2, v2 = diff_mixer(hs, pos_s, cache_k[j], cache_v[j], past_pos, diff_w_in[j], diff_w_out[j],
                                    diff_lambda[j], diff_subln_g[j], lambda_init, rel_bias)
            kp.append(k1)
            vp.append(v1)
            ksl.append(k2)
            vsl.append(v2)
        xp = xp + rms_norm(mp, g[1])
        xs = xs + rms_norm(ms, g[1])
        xp = xp + rms_norm(swiglu(rms_norm(xp, g[2]), ffn_w_gu[i], ffn_w_down[i]), g[3])
        xs = xs + rms_norm(swiglu(rms_norm(xs, g[2]), ffn_w_gu[i], ffn_w_down[i]), g[3])

    return (xp, xs, jnp.stack(ret_p), jnp.stack(ret_s), jnp.stack(kp), jnp.stack(vp),
            jnp.stack(ksl), jnp.stack(vsl))
```

```python
import functools
import math

import numpy as np
import jax
import jax.numpy as jnp
from jax import lax
from jax.experimental import pallas as pl
from jax.experimental.pallas import tpu as pltpu

F32 = jnp.float32
BF16 = jnp.bfloat16

NORM_EPS = 1e-6
ROPE_BASE = 10000.0
CHUNK = 64
N_BUCKETS = 32
MAX_DISTANCE = 128
N_MIXERS = 2
LOG2E = math.log2(math.e)
MASKED = -1e30

V7X_VMEM_BYTES = 64 * 1024 * 1024
VMEM_HEADROOM = 6 * 1024 * 1024
LANE = 128

RET_CHUNK = 256
BIAS_VEC_LEN = 512


def _cparams(semantics, vmem_estimate):
    limit = min(V7X_VMEM_BYTES - VMEM_HEADROOM, max(32 * 1024 * 1024, int(vmem_estimate * 1.25)))
    return pltpu.CompilerParams(dimension_semantics=semantics, vmem_limit_bytes=limit)


def _nbytes(shape, dtype):
    return int(np.prod(shape)) * jnp.dtype(dtype).itemsize


def _pick_tile(n, want):
    t = min(n, want)
    while n % t:
        t //= 2
    return t


def _rms(x, g):
    ms = jnp.mean(x * x, axis=-1, keepdims=True)
    return x * lax.rsqrt(ms + NORM_EPS) * g


def _norm_matmul_body(x_ref, g_ref, w_ref, *rest, segs, n_col_tiles):
    out_refs = rest[:len(segs)]
    h_ref = rest[len(segs)]
    j = pl.program_id(1)

    @pl.when(j == 0)
    def _():
        h_ref[...] = _rms(x_ref[...], g_ref[...]).astype(BF16)

    y = jnp.dot(h_ref[...], w_ref[...], preferred_element_type=F32)
    for o_ref, (lo, hi) in zip(out_refs, segs):
        if lo == 0 and hi == n_col_tiles:
            o_ref[...] = y.astype(o_ref.dtype)
        else:
            @pl.when((j >= lo) & (j < hi))
            def _(o_ref=o_ref):
                o_ref[...] = y.astype(o_ref.dtype)


def _norm_matmul(x, g, w, outs, *, tm_want=512, tn_want=1024, name):
    T, D = x.shape
    N = w.shape[1]
    tm = _pick_tile(T, tm_want)
    tn = tn_want
    for lo, hi, _ in outs:
        tn = math.gcd(tn, math.gcd(lo, hi))
    n_col_tiles = N // tn
    segs = tuple((lo // tn, hi // tn) for lo, hi, _ in outs)

    def out_map(lo, hi):
        return lambda i, j: (i, jnp.clip(j - lo, 0, hi - lo - 1))

    out_specs = [pl.BlockSpec((tm, tn), out_map(lo, hi)) for lo, hi in segs]
    out_shape = [jax.ShapeDtypeStruct((T, hi - lo), dt) for lo, hi, dt in outs]
    vmem = (2 * _nbytes((tm, D), F32) + _nbytes((tm, D), BF16) + 2 * _nbytes((D, tn), BF16)
            + sum(2 * _nbytes((tm, tn), dt) for _, _, dt in outs) + _nbytes((tm, tn), F32))
    return pl.pallas_call(
        functools.partial(_norm_matmul_body, segs=segs, n_col_tiles=n_col_tiles),
        out_shape=out_shape,
        grid=(T // tm, n_col_tiles),
        in_specs=[
            pl.BlockSpec((tm, D), lambda i, j: (i, 0)),
            pl.BlockSpec((1, D), lambda i, j: (0, 0)),
            pl.BlockSpec((D, tn), lambda i, j: (0, j)),
        ],
        out_specs=out_specs,
        scratch_shapes=[pltpu.VMEM((tm, D), BF16)],
        compiler_params=_cparams(("parallel", "arbitrary"), vmem),
        name=name,
    )(x, g.reshape(1, D), w)


def _proj_norm_res_body(m_ref, w_ref, x_ref, g_ref, o_ref):
    y = jnp.dot(m_ref[...], w_ref[...], preferred_element_type=F32)
    o_ref[...] = x_ref[...] + _rms(y, g_ref[...])


def _proj_norm_res(m, w, x, g, *, tm_want=512, name):
    T, D = x.shape
    K = m.shape[1]
    tm = _pick_tile(T, tm_want)
    vmem = (2 * _nbytes((tm, K), BF16) + 2 * _nbytes((K, D), BF16) + 5 * _nbytes((tm, D), F32))
    return pl.pallas_call(
        _proj_norm_res_body,
        out_shape=jax.ShapeDtypeStruct((T, D), F32),
        grid=(T // tm,),
        in_specs=[
            pl.BlockSpec((tm, K), lambda i: (i, 0)),
            pl.BlockSpec((K, D), lambda i: (0, 0)),
            pl.BlockSpec((tm, D), lambda i: (i, 0)),
            pl.BlockSpec((1, D), lambda i: (0, 0)),
        ],
        out_specs=pl.BlockSpec((tm, D), lambda i: (i, 0)),
        compiler_params=_cparams(("parallel",), vmem),
        name=name,
    )(m, w, x, g.reshape(1, D))


def _ffn_body(x_ref, gpre_ref, wg_ref, wu_ref, wd_ref, gpost_ref, o_ref, h_ref, acc_ref):
    f = pl.program_id(1)

    @pl.when(f == 0)
    def _():
        h_ref[...] = _rms(x_ref[...], gpre_ref[...]).astype(BF16)

    h = h_ref[...]
    gate = jnp.dot(h, wg_ref[...], preferred_element_type=F32)
    up = jnp.dot(h, wu_ref[...], preferred_element_type=F32)
    act = (gate * jax.nn.sigmoid(gate) * up).astype(BF16)
    part = jnp.dot(act, wd_ref[...], preferred_element_type=F32)

    @pl.when(f == 0)
    def _():
        acc_ref[...] = part

    @pl.when(f > 0)
    def _():
        acc_ref[...] += part

    @pl.when(f == pl.num_programs(1) - 1)
    def _():
        o_ref[...] = x_ref[...] + _rms(acc_ref[...], gpost_ref[...])


def _ffn(x, g_pre, w_gu, w_down, g_post, *, tm_want=512, tf_want=512, name):
    T, D = x.shape
    F = w_down.shape[0]
    tm = _pick_tile(T, tm_want)
    tf = _pick_tile(F, tf_want)
    nf = F // tf
    vmem = (4 * _nbytes((tm, D), F32) + _nbytes((tm, D), BF16) + _nbytes((tm, D), F32)
            + 4 * _nbytes((D, tf), BF16) + 2 * _nbytes((tf, D), BF16) + 4 * _nbytes((tm, tf), F32))
    return pl.pallas_call(
        _ffn_body,
        out_shape=jax.ShapeDtypeStruct((T, D), F32),
        grid=(T // tm, nf),
        in_specs=[
            pl.BlockSpec((tm, D), lambda i, f: (i, 0)),
            pl.BlockSpec((1, D), lambda i, f: (0, 0)),
            pl.BlockSpec((D, tf), lambda i, f: (0, f)),
            pl.BlockSpec((D, tf), lambda i, f: (0, f + nf)),
            pl.BlockSpec((tf, D), lambda i, f: (f, 0)),
            pl.BlockSpec((1, D), lambda i, f: (0, 0)),
        ],
        out_specs=pl.BlockSpec((tm, D), lambda i, f: (i, 0)),
        scratch_shapes=[pltpu.VMEM((tm, D), BF16), pltpu.VMEM((tm, D), F32)],
        compiler_params=_cparams(("parallel", "arbitrary"), vmem),
        name=name,
    )(x, g_pre.reshape(1, D), w_gu, w_gu, w_down, g_post.reshape(1, D))


def _rotary(x, cos, sin):
    half = x.shape[-1] // 2
    x1, x2 = x[:, :half], x[:, half:]
    return jnp.concatenate([x1 * cos - x2 * sin, x1 * sin + x2 * cos], axis=-1)


def _retention_body(lg_ref, q_ref, k_ref, v_ref, g_ref, cos_ref, sin_ref, gn_ref, *rest,
                    has_state, k_scale):
    if has_state:
        s0_ref, o_ref, sout_ref, s_ref, dmat_ref, qdec_ref, kdec_ref = rest
    else:
        o_ref, sout_ref, s_ref, dmat_ref, qdec_ref, kdec_ref = rest
    h = pl.program_id(1)
    c = pl.program_id(2)
    L, dk = q_ref.shape
    lg = lg_ref[h]

    @pl.when(c == 0)
    def _():
        if has_state:
            s_ref[...] = s0_ref[0, 0]
        else:
            s_ref[...] = jnp.zeros_like(s_ref)
        row = lax.broadcasted_iota(jnp.int32, (L, L), 0)
        col = lax.broadcasted_iota(jnp.int32, (L, L), 1)
        rel = (row - col).astype(F32)
        dmat_ref[...] = jnp.where(row >= col, jnp.exp(rel * lg), 0.0)
        idx = lax.broadcasted_iota(jnp.int32, (L, dk), 0).astype(F32)
        qdec_ref[...] = jnp.exp((idx + 1.0) * lg)
        kdec_ref[...] = jnp.exp((L - 1.0 - idx) * lg)

    cos = cos_ref[...]
    sin = sin_ref[...]
    qr = _rotary(q_ref[...].astype(F32), cos, sin)
    kr = _rotary(k_ref[...].astype(F32), cos, sin) * k_scale
    v = v_ref[...]
    qb = qr.astype(BF16)
    state = s_ref[...]

    scores = lax.dot_general(qb, kr.astype(BF16), (((1,), (1,)), ((), ())),
                             preferred_element_type=F32) * dmat_ref[...]
    o = (jnp.dot(scores.astype(BF16), v, preferred_element_type=F32)
         + jnp.dot(qb, state.astype(BF16), preferred_element_type=F32) * qdec_ref[...])
    kd = (kr * kdec_ref[...]).astype(BF16)
    chunk_dec = jnp.exp(jnp.full((1, state.shape[1]), L, F32) * lg)
    new_state = state * chunk_dec + lax.dot_general(kd, v, (((0,), (0,)), ((), ())),
                                                    preferred_element_type=F32)
    s_ref[...] = new_state

    gate = g_ref[...].astype(F32)
    o_ref[...] = (gate * jax.nn.sigmoid(gate) * _rms(o, gn_ref[...])).astype(o_ref.dtype)

    @pl.when(c == pl.num_programs(2) - 1)
    def _():
        sout_ref[0, 0] = new_state


def _retention(qkvg, cos, sin, gn_g, lgs, state, *, batch, heads, name):
    T, D4 = qkvg.shape
    D = D4 // 4
    dk = D // heads
    S = T // batch
    L = _pick_tile(S, RET_CHUNK)
    nc = S // L
    has_state = state is not None

    def col(off):
        return lambda b, h, c: (b * nc + c, off * heads + h)

    in_specs = [
        pl.BlockSpec(memory_space=pltpu.SMEM),
        pl.BlockSpec((L, dk), col(0)),
        pl.BlockSpec((L, dk), col(1)),
        pl.BlockSpec((L, dk), col(2)),
        pl.BlockSpec((L, dk), col(3)),
        pl.BlockSpec((L, dk // 2), lambda b, h, c: (c, 0)),
        pl.BlockSpec((L, dk // 2), lambda b, h, c: (c, 0)),
        pl.BlockSpec((1, dk), lambda b, h, c: (0, h)),
    ]
    args = [lgs, qkvg, qkvg, qkvg, qkvg, cos, sin, gn_g.reshape(1, D)]
    if has_state:
        in_specs.append(pl.BlockSpec((1, 1, dk, dk), lambda b, h, c: (b, h, 0, 0)))
        args.append(state)
    vmem = (8 * _nbytes((L, dk), BF16) + 4 * _nbytes((L, dk // 2), F32) + 2 * _nbytes((L, dk), BF16)
            + 5 * _nbytes((dk, dk), F32) + _nbytes((L, L), F32) + 12 * _nbytes((L, dk), F32))
    return pl.pallas_call(
        functools.partial(_retention_body, has_state=has_state, k_scale=dk ** -0.5),
        out_shape=[jax.ShapeDtypeStruct((T, D), BF16),
                   jax.ShapeDtypeStruct((batch, heads, dk, dk), F32)],
        grid=(batch, heads, nc),
        in_specs=in_specs,
        out_specs=[pl.BlockSpec((L, dk), lambda b, h, c: (b * nc + c, h)),
                   pl.BlockSpec((1, 1, dk, dk), lambda b, h, c: (b, h, 0, 0))],
        scratch_shapes=[pltpu.VMEM((dk, dk), F32), pltpu.VMEM((L, L), F32),
                        pltpu.VMEM((L, dk), F32), pltpu.VMEM((L, dk), F32)],
        compiler_params=_cparams(("parallel", "parallel", "arbitrary"), vmem),
        name=name,
    )(*args)


def _t5_bucket(rel):
    nb = N_BUCKETS // 2
    max_exact = nb // 2
    bucket = jnp.where(rel > 0, nb, 0)
    n = jnp.abs(rel)
    large = max_exact + (jnp.log(jnp.maximum(n, 1).astype(F32) / max_exact)
                         / math.log(MAX_DISTANCE / max_exact) * (nb - max_exact)).astype(jnp.int32)
    large = jnp.minimum(large, nb - 1)
    return bucket + jnp.where(n < max_exact, n, large)


def _bias_vec_body(bucket_ref, rb_ref, o_ref):
    bkt = bucket_ref[...]
    n_buckets, heads = rb_ref.shape
    for h in range(heads):
        acc = jnp.zeros(bkt.shape, F32)
        for b in range(n_buckets):
            acc = acc + jnp.where(bkt == b, rb_ref[b, h], 0.0)
        o_ref[h:h + 1, :] = acc


def _bias_vec(rel_bias):
    heads = rel_bias.shape[1]
    rel = jnp.arange(BIAS_VEC_LEN, dtype=jnp.int32) - (BIAS_VEC_LEN - CHUNK)
    bucket = _t5_bucket(rel).reshape(1, BIAS_VEC_LEN)
    return pl.pallas_call(
        _bias_vec_body,
        out_shape=jax.ShapeDtypeStruct((heads, BIAS_VEC_LEN), F32),
        in_specs=[pl.BlockSpec(memory_space=pltpu.VMEM), pl.BlockSpec(memory_space=pltpu.SMEM)],
        out_specs=pl.BlockSpec(memory_space=pltpu.VMEM),
        name="t5_bias_lookup",
    )(bucket, rel_bias.astype(F32))


def _toeplitz(vec, offs, tq, tk):
    heads = vec.shape[0]
    W = tq + tk + 1
    r0 = BIAS_VEC_LEN - CHUNK
    u = np.arange(W)[None, :] - tq - np.asarray(offs)[:, None]
    idx = np.clip(u + r0, 0, BIAS_VEC_LEN - 1).astype(np.int32)
    win = vec[:, idx]
    n = len(offs)
    flat = jnp.tile(win, (1, 1, tq))[:, :, :tq * (W - 1)]
    skew = flat.reshape(heads, n, tq, W - 1)
    return skew[:, :, :, tq:tq + tk]


def _visible(q_pos, k_pos):
    return (k_pos[None, :] // CHUNK) <= (q_pos[:, None] // CHUNK)


def _lambda(lam_ref, lambda_init):
    lv = lam_ref[...]
    a = jnp.sum(lv[0:1] * lv[1:2], axis=-1, keepdims=True)
    b = jnp.sum(lv[2:3] * lv[3:4], axis=-1, keepdims=True)
    return jnp.exp(a) - jnp.exp(b) + lambda_init


def _diff_finish(acc1, l1, acc2, l2, lam, g, lambda_init):
    o = acc1 * (1.0 / l1) - lam * (acc2 * (1.0 / l2))
    return _rms(o, g) * (1.0 - lambda_init)


def _diff_prompt_body(qb_ref, kb_ref, near_ref, first_ref, last_ref,
                      q_ref, k_ref, v_ref, bias_ref, cfar_ref, lam_ref, g_ref, o_ref,
                      m_ref, l_ref, acc_ref, *, logit_scale, lambda_init):
    h = pl.program_id(1)
    s = pl.program_id(2)
    dh = q_ref.shape[1] // 2
    near = near_ref[s]

    @pl.when(first_ref[s] == 1)
    def _():
        m_ref[...] = jnp.full_like(m_ref, MASKED)
        l_ref[...] = jnp.zeros_like(l_ref)
        acc_ref[...] = jnp.zeros_like(acc_ref)

    def update(bias, shift):
        v = v_ref[...]
        for j in range(2):
            q = q_ref[:, j * dh:(j + 1) * dh]
            k = k_ref[:, j * dh:(j + 1) * dh]
            t = lax.dot_general(q, k, (((1,), (1,)), ((), ())),
                                preferred_element_type=F32) * logit_scale
            if bias is not None:
                t = t + bias
            m_old = m_ref[j]
            m_new = jnp.maximum(m_old, jnp.max(t, axis=-1, keepdims=True) + shift)
            p = jnp.exp2(t - (m_new - shift))
            alpha = jnp.exp2(m_old - m_new)
            l_ref[j] = alpha * l_ref[j] + jnp.sum(p, axis=-1, keepdims=True)
            acc_ref[j] = alpha * acc_ref[j] + jnp.dot(p.astype(BF16), v, preferred_element_type=F32)
            m_ref[j] = m_new

    @pl.when(near >= 0)
    def _():
        update(bias_ref[0, jnp.maximum(near, 0)], 0.0)

    @pl.when(near < 0)
    def _():
        update(None, cfar_ref[h] * LOG2E)

    @pl.when(last_ref[s] == 1)
    def _():
        lam = _lambda(lam_ref, lambda_init)
        o_ref[...] = _diff_finish(acc_ref[0], l_ref[0], acc_ref[1], l_ref[1], lam, g_ref[...],
                                  lambda_init).astype(o_ref.dtype)


def _diff_attention_prompt(q, k, v, vec, lam_vecs, subln_g, *, batch, heads, lambda_init,
                           tq_want=512, tk_want=512, name):
    T, D = q.shape
    S = T // batch
    dv = D // heads
    tq = _pick_tile(S, tq_want)
    tk = _pick_tile(tq, tk_want)
    assert tq % CHUNK == 0 and tk % CHUNK == 0
    nq, nk = S // tq, S // tk

    pairs = [(a, b) for a in range(nq) for b in range(nk) if b * tk < (a + 1) * tq]
    offs = sorted({a * tq - b * tk for a, b in pairs if a * tq - b * tk - (tk - 1) <= MAX_DISTANCE})
    near_of = {o: t for t, o in enumerate(offs)}
    qb = np.array([a for a, _ in pairs], np.int32)
    kb = np.array([b for _, b in pairs], np.int32)
    near = np.array([near_of.get(a * tq - b * tk, -1) for a, b in pairs], np.int32)
    first = np.array([int(b == 0) for _, b in pairs], np.int32)
    last = np.array([int(i + 1 == len(pairs) or pairs[i + 1][0] != a)
                     for i, (a, _) in enumerate(pairs)], np.int32)

    tiles = _toeplitz(vec, offs, tq, tk) * LOG2E
    qi = jnp.arange(tq, dtype=jnp.int32)
    kj = jnp.arange(tk, dtype=jnp.int32)
    vis = jnp.stack([_visible(qi + max(o, 0), kj + max(-o, 0)) for o in offs])
    tiles = jnp.where(vis[None], tiles, MASKED)
    cfar = vec[:, 0]
    n_near = len(offs)

    def qmap(b, h, s, qb, kb, near, first, last):
        return (b * nq + qb[s], h)

    def kmap(b, h, s, qb, kb, near, first, last):
        return (b * nk + kb[s], h)

    vmem = (4 * _nbytes((tq, dv), BF16) + 8 * _nbytes((tk, dv), BF16)
            + 2 * _nbytes((n_near, tq, tk), F32) + 2 * _nbytes((tq, dv), F32)
            + 4 * _nbytes((tq, LANE), F32) + 8 * _nbytes((tq, tk), F32))
    grid_spec = pltpu.PrefetchScalarGridSpec(
        num_scalar_prefetch=5,
        grid=(batch, heads, len(pairs)),
        in_specs=[
            pl.BlockSpec((tq, dv), qmap),
            pl.BlockSpec((tk, dv), kmap),
            pl.BlockSpec((tk, dv), kmap),
            pl.BlockSpec((1, n_near, tq, tk), lambda b, h, s, *_: (h, 0, 0, 0)),
            pl.BlockSpec(memory_space=pltpu.SMEM),
            pl.BlockSpec(lam_vecs.shape, lambda b, h, s, *_: (0, 0)),
            pl.BlockSpec((1, dv), lambda b, h, s, *_: (0, 0)),
        ],
        out_specs=pl.BlockSpec((tq, dv), qmap),
        scratch_shapes=[pltpu.VMEM((2, tq, 1), F32), pltpu.VMEM((2, tq, 1), F32),
                        pltpu.VMEM((2, tq, dv), F32)],
    )
    return pl.pallas_call(
        functools.partial(_diff_prompt_body, logit_scale=(dv // 2) ** -0.5 * LOG2E,
                          lambda_init=lambda_init),
        out_shape=jax.ShapeDtypeStruct((T, D), BF16),
        grid_spec=grid_spec,
        compiler_params=_cparams(("parallel", "parallel", "arbitrary"), vmem),
        name=name,
    )(qb, kb, near, first, last, q, k, v, tiles, cfar, lam_vecs, subln_g.reshape(1, dv))


def _diff_sample_body(q_ref, kc_ref, vc_ref, kn_ref, vn_ref, bc_ref, bn_ref, lam_ref, g_ref, o_ref,
                      *, logit_scale, lambda_init):
    dh = q_ref.shape[1] // 2
    vc = vc_ref[...].astype(BF16)
    vn = vn_ref[...]
    nt = (((1,), (1,)), ((), ()))
    accs, ls = [], []
    for j in range(2):
        q = q_ref[:, j * dh:(j + 1) * dh]
        kc = kc_ref[:, j * dh:(j + 1) * dh].astype(BF16)
        kn = kn_ref[:, j * dh:(j + 1) * dh]
        tc = lax.dot_general(q, kc, nt, preferred_element_type=F32) * logit_scale + bc_ref[0]
        tn = lax.dot_general(q, kn, nt, preferred_element_type=F32) * logit_scale + bn_ref[0]
        m = jnp.maximum(jnp.max(tc, axis=-1, keepdims=True), jnp.max(tn, axis=-1, keepdims=True))
        pc = jnp.exp2(tc - m)
        pn = jnp.exp2(tn - m)
        ls.append(jnp.sum(pc, axis=-1, keepdims=True) + jnp.sum(pn, axis=-1, keepdims=True))
        accs.append(jnp.dot(pc.astype(BF16), vc, preferred_element_type=F32)
                    + jnp.dot(pn.astype(BF16), vn, preferred_element_type=F32))
    lam = _lambda(lam_ref, lambda_init)
    o_ref[...] = _diff_finish(accs[0], ls[0], accs[1], ls[1], lam, g_ref[...],
                              lambda_init).astype(o_ref.dtype)


def _diff_attention_sample(q, k_new, v_new, k_cache, v_cache, vec, lam_vecs, subln_g, *,
                           batch, heads, lambda_init, name):
    T, D = q.shape
    n = T // batch
    P = k_cache.shape[0] // batch
    dv = D // heads
    q_pos = P + jnp.arange(n, dtype=jnp.int32)
    bc = _toeplitz(vec, [P], n, P)[:, 0] * LOG2E
    bc = jnp.where(_visible(q_pos, jnp.arange(P, dtype=jnp.int32))[None], bc, MASKED)
    bn = _toeplitz(vec, [0], n, n)[:, 0] * LOG2E
    bn = jnp.where(_visible(q_pos, q_pos)[None], bn, MASKED)

    row = lambda b, h: (b, h)
    vmem = (4 * _nbytes((P, dv), F32) + 2 * _nbytes((P, dv), BF16) + 2 * _nbytes((n, P), F32)
            + 6 * _nbytes((n, P), F32) + 12 * _nbytes((n, dv), F32))
    return pl.pallas_call(
        functools.partial(_diff_sample_body, logit_scale=(dv // 2) ** -0.5 * LOG2E,
                          lambda_init=lambda_init),
        out_shape=jax.ShapeDtypeStruct((T, D), BF16),
        grid=(batch, heads),
        in_specs=[
            pl.BlockSpec((n, dv), row),
            pl.BlockSpec((P, dv), row),
            pl.BlockSpec((P, dv), row),
            pl.BlockSpec((n, dv), row),
            pl.BlockSpec((n, dv), row),
            pl.BlockSpec((1, n, P), lambda b, h: (h, 0, 0)),
            pl.BlockSpec((1, n, n), lambda b, h: (h, 0, 0)),
            pl.BlockSpec(lam_vecs.shape, lambda b, h: (0, 0)),
            pl.BlockSpec((1, dv), lambda b, h: (0, 0)),
        ],
        out_specs=pl.BlockSpec((n, dv), row),
        compiler_params=_cparams(("parallel", "parallel"), vmem),
        name=name,
    )(q, k_cache, v_cache, k_new, v_new, bc, bn, lam_vecs, subln_g.reshape(1, dv))


def _rope_tables(pos, half):
    inv = ROPE_BASE ** (-jnp.arange(half, dtype=F32) / half)
    ang = pos.astype(F32)[:, None] * inv[None, :]
    return jnp.cos(ang), jnp.sin(ang)


def kernel(x_prompt, x_sample, state_ret, cache_k, cache_v, norm_g, ret_w_in, ret_w_out, ret_gn_g,
           diff_w_in, diff_w_out, diff_lambda, diff_subln_g, rel_bias, ffn_w_gu, ffn_w_down):
    B, S, D = x_prompt.shape
    DB, n_s, _ = x_sample.shape
    depth = norm_g.shape[0]
    ret_heads = state_ret.shape[2]
    diff_heads = cache_k.shape[3]
    past = cache_k.shape[2]

    pos_p = jnp.arange(S, dtype=jnp.int32)
    pos_s = past + jnp.arange(n_s, dtype=jnp.int32)
    half = D // ret_heads // 2
    rope_p = _rope_tables(pos_p, half)
    rope_s = _rope_tables(pos_s, half)
    lgs = jnp.log1p(-jnp.exp2(-5.0 - jnp.arange(ret_heads, dtype=F32)))
    bias_vec = _bias_vec(rel_bias)

    xp = x_prompt.reshape(B * S, D)
    xs = x_sample.reshape(DB * n_s, D)
    ret_p, ret_s, kp, vp, ks, vs = [], [], [], [], [], []
    for i in range(depth):
        g = norm_g[i]
        j = i // N_MIXERS
        if i % N_MIXERS == 0:
            w_in = ret_w_in[j].astype(BF16)
            w_out = ret_w_out[j].astype(BF16)
            segs = [(0, 4 * D, BF16)]
            (qp,) = _norm_matmul(xp, g[0], w_in, segs, tm_want=1024, name=f"ret_in_p{i}")
            (qs,) = _norm_matmul(xs, g[0], w_in, segs, tm_want=1024, name=f"ret_in_s{i}")
            mp, sp = _retention(qp, *rope_p, ret_gn_g[j], lgs, None, batch=B, heads=ret_heads,
                                name=f"retention_p{i}")
            ms, ss = _retention(qs, *rope_s, ret_gn_g[j], lgs, state_ret[j], batch=DB,
                                heads=ret_heads, name=f"retention_s{i}")
            ret_p.append(sp)
            ret_s.append(ss)
        else:
            lambda_init = 0.8 - 0.6 * math.exp(-0.3 * i)
            w_in = diff_w_in[j].astype(BF16)
            w_out = diff_w_out[j].astype(BF16)
            segs = [(0, D, BF16), (D, 2 * D, F32), (D, 2 * D, BF16), (2 * D, 3 * D, F32),
                    (2 * D, 3 * D, BF16)]
            q1, k1f, k1, v1f, v1 = _norm_matmul(xp, g[0], w_in, segs, name=f"diff_in_p{i}")
            q2, k2f, k2, v2f, v2 = _norm_matmul(xs, g[0], w_in, segs, name=f"diff_in_s{i}")
            mp = _diff_attention_prompt(q1, k1, v1, bias_vec, diff_lambda[j], diff_subln_g[j],
                                        batch=B, heads=diff_heads, lambda_init=lambda_init,
                                        name=f"diff_attn_p{i}")
            ms = _diff_attention_sample(q2, k2, v2, cache_k[j].reshape(DB * past, D),
                                        cache_v[j].reshape(DB * past, D), bias_vec, diff_lambda[j],
                                        diff_subln_g[j], batch=DB, heads=diff_heads,
                                        lambda_init=lambda_init, name=f"diff_attn_s{i}")
            kp.append(k1f)
            vp.append(v1f)
            ks.append(k2f)
            vs.append(v2f)
        xp = _proj_norm_res(mp, w_out, xp, g[1], name=f"mix_out_p{i}")
        xs = _proj_norm_res(ms, w_out, xs, g[1], name=f"mix_out_s{i}")
        w_gu = ffn_w_gu[i].astype(BF16)
        w_down = ffn_w_down[i].astype(BF16)
        xp = _ffn(xp, g[2], w_gu, w_down, g[3], name=f"ffn_p{i}")
        xs = _ffn(xs, g[2], w_gu, w_down, g[3], name=f"ffn_s{i}")

    dvh = D // diff_heads
    return (xp.reshape(B, S, D), xs.reshape(DB, n_s, D),
            jnp.stack(ret_p), jnp.stack(ret_s),
            jnp.stack(kp).reshape(len(kp), B, S, diff_heads, dvh),
            jnp.stack(vp).reshape(len(vp), B, S, diff_heads, dvh),
            jnp.stack(ks).reshape(len(ks), DB, n_s, diff_heads, dvh),
            jnp.stack(vs).reshape(len(vs), DB, n_s, diff_heads, dvh))
```

```python
import functools
import math

import numpy as np
import jax
import jax.numpy as jnp
from jax import lax
from jax.experimental import pallas as pl
from jax.experimental.pallas import tpu as pltpu

F32 = jnp.float32
BF16 = jnp.bfloat16

NORM_EPS = 1e-6
ROPE_BASE = 10000.0
CHUNK = 64
N_BUCKETS = 32
MAX_DISTANCE = 128
N_MIXERS = 2
LOG2E = math.log2(math.e)
MASKED = -1e30

V7X_VMEM_BYTES = 64 * 1024 * 1024
VMEM_HEADROOM = 6 * 1024 * 1024
LANE = 128

RET_CHUNK = 256
BIAS_VEC_LEN = 512


def _cparams(semantics, vmem_estimate):
    limit = min(V7X_VMEM_BYTES - VMEM_HEADROOM, max(32 * 1024 * 1024, int(vmem_estimate * 1.25)))
    return pltpu.CompilerParams(dimension_semantics=semantics, vmem_limit_bytes=limit)


def _nbytes(shape, dtype):
    return int(np.prod(shape)) * jnp.dtype(dtype).itemsize


def _pick_tile(n, want):
    t = min(n, want)
    while n % t:
        t //= 2
    return t


def _rms(x, g):
    ms = jnp.mean(x * x, axis=-1, keepdims=True)
    return x * lax.rsqrt(ms + NORM_EPS) * g


def _norm_matmul_body(x_ref, g_ref, w_ref, o_ref, h_ref):
    @pl.when(pl.program_id(1) == 0)
    def _():
        h_ref[...] = _rms(x_ref[...], g_ref[...]).astype(BF16)

    o_ref[...] = jnp.dot(h_ref[...], w_ref[...], preferred_element_type=F32).astype(o_ref.dtype)


def _norm_matmul(x, g, w, *, tm_want=1024, tn_want=1024, name):
    T, D = x.shape
    N = w.shape[1]
    tm = _pick_tile(T, tm_want)
    tn = _pick_tile(N, tn_want)
    vmem = (2 * _nbytes((tm, D), F32) + _nbytes((tm, D), BF16) + 2 * _nbytes((D, tn), BF16)
            + 2 * _nbytes((tm, tn), BF16) + _nbytes((tm, tn), F32))
    return pl.pallas_call(
        _norm_matmul_body,
        out_shape=jax.ShapeDtypeStruct((T, N), BF16),
        grid=(T // tm, N // tn),
        in_specs=[
            pl.BlockSpec((tm, D), lambda i, j: (i, 0)),
            pl.BlockSpec((1, D), lambda i, j: (0, 0)),
            pl.BlockSpec((D, tn), lambda i, j: (0, j)),
        ],
        out_specs=pl.BlockSpec((tm, tn), lambda i, j: (i, j)),
        scratch_shapes=[pltpu.VMEM((tm, D), BF16)],
        compiler_params=_cparams(("parallel", "arbitrary"), vmem),
        name=name,
    )(x, g.reshape(1, D), w)


def _diff_in_body(x_ref, g_ref, w_ref, qkv_ref, kf_ref, vf_ref, h_ref, *, q_scale):
    j = pl.program_id(1)

    def project():
        return jnp.dot(h_ref[...], w_ref[...], preferred_element_type=F32)

    @pl.when(j == 0)
    def _():
        h_ref[...] = _rms(x_ref[...], g_ref[...]).astype(BF16)
        qkv_ref[...] = (project() * q_scale).astype(BF16)

    @pl.when(j == 1)
    def _():
        y = project()
        qkv_ref[...] = y.astype(BF16)
        kf_ref[...] = y.reshape(kf_ref.shape)

    @pl.when(j == 2)
    def _():
        y = project()
        qkv_ref[...] = y.astype(BF16)
        vf_ref[...] = y.reshape(vf_ref.shape)


def _diff_in(x, g, w, *, heads, tm_want=512, name):
    T, D = x.shape
    dv = D // heads
    tm = _pick_tile(T, tm_want)
    rows = jax.ShapeDtypeStruct((T, heads, dv), F32)
    vmem = (2 * _nbytes((tm, D), F32) + _nbytes((tm, D), BF16) + 2 * _nbytes((D, D), BF16)
            + 2 * _nbytes((tm, D), BF16) + 4 * _nbytes((tm, D), F32) + 2 * _nbytes((tm, D), F32))
    return pl.pallas_call(
        functools.partial(_diff_in_body, q_scale=(dv // 2) ** -0.5 * LOG2E),
        out_shape=[jax.ShapeDtypeStruct((T, 3 * D), BF16), rows, rows],
        grid=(T // tm, 3),
        in_specs=[
            pl.BlockSpec((tm, D), lambda i, j: (i, 0)),
            pl.BlockSpec((1, D), lambda i, j: (0, 0)),
            pl.BlockSpec((D, D), lambda i, j: (0, j)),
        ],
        out_specs=[pl.BlockSpec((tm, D), lambda i, j: (i, j)),
                   pl.BlockSpec((tm, heads, dv), lambda i, j: (i, 0, 0)),
                   pl.BlockSpec((tm, heads, dv), lambda i, j: (i, 0, 0))],
        scratch_shapes=[pltpu.VMEM((tm, D), BF16)],
        compiler_params=_cparams(("parallel", "arbitrary"), vmem),
        name=name,
    )(x, g.reshape(1, D), w)


def _proj_norm_res_body(m_ref, w_ref, x_ref, g_ref, o_ref):
    y = jnp.dot(m_ref[...], w_ref[...], preferred_element_type=F32)
    o_ref[...] = x_ref[...] + _rms(y, g_ref[...])


def _proj_norm_res(m, w, x, g, *, tm_want=512, name):
    T, D = x.shape
    K = m.shape[1]
    tm = _pick_tile(T, tm_want)
    vmem = (2 * _nbytes((tm, K), BF16) + 2 * _nbytes((K, D), BF16) + 5 * _nbytes((tm, D), F32))
    return pl.pallas_call(
        _proj_norm_res_body,
        out_shape=jax.ShapeDtypeStruct((T, D), F32),
        grid=(T // tm,),
        in_specs=[
            pl.BlockSpec((tm, K), lambda i: (i, 0)),
            pl.BlockSpec((K, D), lambda i: (0, 0)),
            pl.BlockSpec((tm, D), lambda i: (i, 0)),
            pl.BlockSpec((1, D), lambda i: (0, 0)),
        ],
        out_specs=pl.BlockSpec((tm, D), lambda i: (i, 0)),
        compiler_params=_cparams(("parallel",), vmem),
        name=name,
    )(m, w, x, g.reshape(1, D))


def _ffn_body(x_ref, gpre_ref, wg_ref, wu_ref, wd_ref, gpost_ref, o_ref, h_ref, acc_ref):
    f = pl.program_id(1)

    @pl.when(f == 0)
    def _():
        h_ref[...] = _rms(x_ref[...], gpre_ref[...]).astype(BF16)
        acc_ref[...] = jnp.zeros_like(acc_ref)

    h = h_ref[...]
    gate = jnp.dot(h, wg_ref[...], preferred_element_type=F32)
    up = jnp.dot(h, wu_ref[...], preferred_element_type=F32)
    act = (gate * jax.nn.sigmoid(gate) * up).astype(BF16)
    acc_ref[...] += jnp.dot(act, wd_ref[...], preferred_element_type=F32)

    @pl.when(f == pl.num_programs(1) - 1)
    def _():
        o_ref[...] = x_ref[...] + _rms(acc_ref[...], gpost_ref[...])


def _ffn(x, g_pre, w_gu, w_down, g_post, *, tm_want=512, tf_want=512, name):
    T, D = x.shape
    F = w_down.shape[0]
    tm = _pick_tile(T, tm_want)
    tf = _pick_tile(F, tf_want)
    nf = F // tf
    vmem = (4 * _nbytes((tm, D), F32) + _nbytes((tm, D), BF16) + _nbytes((tm, D), F32)
            + 4 * _nbytes((D, tf), BF16) + 2 * _nbytes((tf, D), BF16) + 4 * _nbytes((tm, tf), F32))
    return pl.pallas_call(
        _ffn_body,
        out_shape=jax.ShapeDtypeStruct((T, D), F32),
        grid=(T // tm, nf),
        in_specs=[
            pl.BlockSpec((tm, D), lambda i, f: (i, 0)),
            pl.BlockSpec((1, D), lambda i, f: (0, 0)),
            pl.BlockSpec((D, tf), lambda i, f: (0, f)),
            pl.BlockSpec((D, tf), lambda i, f: (0, f + nf)),
            pl.BlockSpec((tf, D), lambda i, f: (f, 0)),
            pl.BlockSpec((1, D), lambda i, f: (0, 0)),
        ],
        out_specs=pl.BlockSpec((tm, D), lambda i, f: (i, 0)),
        scratch_shapes=[pltpu.VMEM((tm, D), BF16), pltpu.VMEM((tm, D), F32)],
        compiler_params=_cparams(("parallel", "arbitrary"), vmem),
        name=name,
    )(x, g_pre.reshape(1, D), w_gu, w_gu, w_down, g_post.reshape(1, D))


def _rotary(x, cos, sin):
    half = x.shape[-1] // 2
    x1, x2 = x[:, :half], x[:, half:]
    return jnp.concatenate([x1 * cos - x2 * sin, x1 * sin + x2 * cos], axis=-1)


def _retention_body(lg_ref, q_ref, k_ref, v_ref, g_ref, cos_ref, sin_ref, gn_ref, *rest,
                    has_state, k_scale):
    if has_state:
        s0_ref, o_ref, sout_ref, s_ref, dmat_ref, qdec_ref, kdec_ref = rest
    else:
        o_ref, sout_ref, s_ref, dmat_ref, qdec_ref, kdec_ref = rest
    h = pl.program_id(1)
    c = pl.program_id(2)
    L, dk = q_ref.shape
    lg = lg_ref[h]

    @pl.when(c == 0)
    def _():
        if has_state:
            s_ref[...] = s0_ref[0, 0]
        else:
            s_ref[...] = jnp.zeros_like(s_ref)
        row = lax.broadcasted_iota(jnp.int32, (L, L), 0)
        col = lax.broadcasted_iota(jnp.int32, (L, L), 1)
        rel = (row - col).astype(F32)
        dmat_ref[...] = jnp.where(row >= col, jnp.exp(rel * lg), 0.0)
        idx = lax.broadcasted_iota(jnp.int32, (L, dk), 0).astype(F32)
        qdec_ref[...] = jnp.exp((idx + 1.0) * lg)
        kdec_ref[...] = jnp.exp((L - 1.0 - idx) * lg)

    cos = cos_ref[...]
    sin = sin_ref[...]
    qr = _rotary(q_ref[...].astype(F32), cos, sin)
    kr = _rotary(k_ref[...].astype(F32), cos, sin) * k_scale
    v = v_ref[...]
    qb = qr.astype(BF16)
    state = s_ref[...]

    scores = lax.dot_general(qb, kr.astype(BF16), (((1,), (1,)), ((), ())),
                             preferred_element_type=F32) * dmat_ref[...]
    o = (jnp.dot(scores.astype(BF16), v, preferred_element_type=F32)
         + jnp.dot(qb, state.astype(BF16), preferred_element_type=F32) * qdec_ref[...])
    kd = (kr * kdec_ref[...]).astype(BF16)
    chunk_dec = jnp.exp(jnp.full((1, state.shape[1]), L, F32) * lg)
    new_state = state * chunk_dec + lax.dot_general(kd, v, (((0,), (0,)), ((), ())),
                                                    preferred_element_type=F32)
    s_ref[...] = new_state

    gate = g_ref[...].astype(F32)
    o_ref[...] = (gate * jax.nn.sigmoid(gate) * _rms(o, gn_ref[...])).astype(o_ref.dtype)

    @pl.when(c == pl.num_programs(2) - 1)
    def _():
        sout_ref[0, 0] = new_state


def _retention(qkvg, cos, sin, gn_g, lgs, state, *, batch, heads, name):
    T, D4 = qkvg.shape
    D = D4 // 4
    dk = D // heads
    S = T // batch
    L = _pick_tile(S, RET_CHUNK)
    nc = S // L
    has_state = state is not None

    def col(off):
        return lambda b, h, c: (b * nc + c, off * heads + h)

    in_specs = [
        pl.BlockSpec(memory_space=pltpu.SMEM),
        pl.BlockSpec((L, dk), col(0)),
        pl.BlockSpec((L, dk), col(1)),
        pl.BlockSpec((L, dk), col(2)),
        pl.BlockSpec((L, dk), col(3)),
        pl.BlockSpec((L, dk // 2), lambda b, h, c: (c, 0)),
        pl.BlockSpec((L, dk // 2), lambda b, h, c: (c, 0)),
        pl.BlockSpec((1, dk), lambda b, h, c: (0, h)),
    ]
    args = [lgs, qkvg, qkvg, qkvg, qkvg, cos, sin, gn_g.reshape(1, D)]
    if has_state:
        in_specs.append(pl.BlockSpec((1, 1, dk, dk), lambda b, h, c: (b, h, 0, 0)))
        args.append(state)
    vmem = (8 * _nbytes((L, dk), BF16) + 4 * _nbytes((L, dk // 2), F32) + 2 * _nbytes((L, dk), BF16)
            + 5 * _nbytes((dk, dk), F32) + _nbytes((L, L), F32) + 12 * _nbytes((L, dk), F32))
    return pl.pallas_call(
        functools.partial(_retention_body, has_state=has_state, k_scale=dk ** -0.5),
        out_shape=[jax.ShapeDtypeStruct((T, D), BF16),
                   jax.ShapeDtypeStruct((batch, heads, dk, dk), F32)],
        grid=(batch, heads, nc),
        in_specs=in_specs,
        out_specs=[pl.BlockSpec((L, dk), lambda b, h, c: (b * nc + c, h)),
                   pl.BlockSpec((1, 1, dk, dk), lambda b, h, c: (b, h, 0, 0))],
        scratch_shapes=[pltpu.VMEM((dk, dk), F32), pltpu.VMEM((L, L), F32),
                        pltpu.VMEM((L, dk), F32), pltpu.VMEM((L, dk), F32)],
        compiler_params=_cparams(("parallel", "parallel", "arbitrary"), vmem),
        name=name,
    )(*args)


def _t5_bucket(rel):
    nb = N_BUCKETS // 2
    max_exact = nb // 2
    bucket = jnp.where(rel > 0, nb, 0)
    n = jnp.abs(rel)
    large = max_exact + (jnp.log(jnp.maximum(n, 1).astype(F32) / max_exact)
                         / math.log(MAX_DISTANCE / max_exact) * (nb - max_exact)).astype(jnp.int32)
    large = jnp.minimum(large, nb - 1)
    return bucket + jnp.where(n < max_exact, n, large)


def _bias_vec_body(bucket_ref, rb_ref, o_ref):
    bkt = bucket_ref[...]
    n_buckets, heads = rb_ref.shape
    for h in range(heads):
        acc = jnp.zeros(bkt.shape, F32)
        for b in range(n_buckets):
            acc = acc + jnp.where(bkt == b, rb_ref[b, h], 0.0)
        o_ref[h:h + 1, :] = acc


def _bias_vec(rel_bias):
    heads = rel_bias.shape[1]
    rel = jnp.arange(BIAS_VEC_LEN, dtype=jnp.int32) - (BIAS_VEC_LEN - CHUNK)
    bucket = _t5_bucket(rel).reshape(1, BIAS_VEC_LEN)
    return pl.pallas_call(
        _bias_vec_body,
        out_shape=jax.ShapeDtypeStruct((heads, BIAS_VEC_LEN), F32),
        in_specs=[pl.BlockSpec(memory_space=pltpu.VMEM), pl.BlockSpec(memory_space=pltpu.SMEM)],
        out_specs=pl.BlockSpec(memory_space=pltpu.VMEM),
        name="t5_bias_lookup",
    )(bucket, rel_bias.astype(F32))


def _bias_tiles_body(vec_ref, o_ref, *, bases):
    _, _, tq, tk = o_ref.shape
    base, width = _bias_window(tq, tk)
    r0 = BIAS_VEC_LEN - CHUNK
    vec = vec_ref[0]
    first = vec[:, 0:1]
    last = vec[:, BIAS_VEC_LEN - 1:BIAS_VEC_LEN]
    padded = vec
    if width > BIAS_VEC_LEN:
        padded = jnp.concatenate([vec, jnp.broadcast_to(last, (1, width - BIAS_VEC_LEN))], axis=1)
    slot = lax.broadcasted_iota(jnp.int32, (1, width), 1)
    row = lax.broadcasted_iota(jnp.int32, (tq, tk), 0)
    col = lax.broadcasted_iota(jnp.int32, (tq, tk), 1)
    for t, (q0, k0) in enumerate(bases):
        start = r0 - base - (q0 - k0)
        idx = slot + start
        window = pltpu.roll(padded, (-start) % width, 1)
        window = jnp.where(idx < 0, first, jnp.where(idx >= BIAS_VEC_LEN, last, window))
        skew = pltpu.roll(jnp.broadcast_to(window, (tq, width)), 0, 1, stride=1, stride_axis=0)
        visible = (col + k0) // CHUNK <= (row + q0) // CHUNK
        o_ref[0, t] = jnp.where(visible, skew[:, base:base + tk] * LOG2E, MASKED)


def _bias_window(tq, tk):
    base = -(-tq // LANE) * LANE
    return base, max(-(-(base + tk) // LANE) * LANE, BIAS_VEC_LEN)


def _bias_tiles(vec, bases, tq, tk, *, name):
    heads = vec.shape[0]
    n = len(bases)
    _, width = _bias_window(tq, tk)
    vmem = 2 * _nbytes((n, tq, tk), F32) + 3 * _nbytes((tq, width), F32) + 4 * _nbytes((tq, tk), F32)
    return pl.pallas_call(
        functools.partial(_bias_tiles_body, bases=tuple(bases)),
        out_shape=jax.ShapeDtypeStruct((heads, n, tq, tk), F32),
        grid=(heads,),
        in_specs=[pl.BlockSpec((1, 1, BIAS_VEC_LEN), lambda h: (h, 0, 0))],
        out_specs=pl.BlockSpec((1, n, tq, tk), lambda h: (h, 0, 0, 0)),
        compiler_params=_cparams(("parallel",), vmem),
        name=name,
    )(vec.reshape(heads, 1, BIAS_VEC_LEN))


def _lambda(lam_ref, lambda_init):
    lv = lam_ref[...]
    a = jnp.sum(lv[0:1] * lv[1:2], axis=-1, keepdims=True)
    b = jnp.sum(lv[2:3] * lv[3:4], axis=-1, keepdims=True)
    return jnp.exp(a) - jnp.exp(b) + lambda_init


def _diff_finish(acc1, l1, acc2, l2, lam, g, lambda_init):
    o = acc1 * (1.0 / l1) - lam * (acc2 * (1.0 / l2))
    return _rms(o, g) * (1.0 - lambda_init)


def _diff_prompt_body(qb_ref, kb_ref, kind_ref, first_ref, last_ref,
                      q_ref, k_ref, v_ref, bias_ref, cfar_ref, lam_ref, g_ref, o_ref,
                      m_ref, l_ref, acc_ref, *, lambda_init, step_kinds, tk):
    h = pl.program_id(1)
    s = pl.program_id(2)
    dh = q_ref.shape[1] // 2
    kind = kind_ref[s]

    @pl.when(first_ref[s] == 1)
    def _():
        m_ref[...] = jnp.full_like(m_ref, MASKED)
        l_ref[...] = jnp.zeros_like(l_ref)
        acc_ref[...] = jnp.zeros_like(acc_ref)

    lane_tiles = [slice(c * LANE, (c + 1) * LANE) for c in range(tk // LANE)]

    def update(sub, tile, r0):
        keys = slice(sub * tk, (sub + 1) * tk)
        rows = slice(r0, q_ref.shape[0])
        v = v_ref[keys, :]
        shift = cfar_ref[h] * LOG2E if tile < 0 else 0.0
        for j in range(2):
            q = q_ref[rows, j * dh:(j + 1) * dh]
            k = k_ref[keys, j * dh:(j + 1) * dh]
            t = lax.dot_general(q, k, (((1,), (1,)), ((), ())), preferred_element_type=F32)
            if tile >= 0:
                t = t + bias_ref[0, tile, rows, :]
            part = t[:, lane_tiles[0]]
            for sl in lane_tiles[1:]:
                part = jnp.maximum(part, t[:, sl])
            m_old = m_ref[j, rows, :]
            m_new = jnp.maximum(m_old, jnp.max(part, axis=-1, keepdims=True) + shift)
            m_sub = m_new - shift
            alpha = jnp.exp2(m_old - m_new)
            ps = [jnp.exp2(t[:, sl] - m_sub) for sl in lane_tiles]
            l_ref[j, rows, :] = alpha * l_ref[j, rows, :] + functools.reduce(lambda a, b: a + b, ps)
            p = jnp.concatenate([x.astype(BF16) for x in ps], axis=1)
            alpha_v = jnp.concatenate([alpha] * (v.shape[1] // LANE), axis=1)
            acc_ref[j, rows, :] = (alpha_v * acc_ref[j, rows, :]
                                   + jnp.dot(p, v, preferred_element_type=F32))
            m_ref[j, rows, :] = m_new

    for t_id, kinds in enumerate(step_kinds):
        @pl.when(kind == t_id)
        def _(kinds=kinds):
            for sub, sub_kind in enumerate(kinds):
                if sub_kind is not None:
                    update(sub, *sub_kind)

    @pl.when(last_ref[s] == 1)
    def _():
        lam = _lambda(lam_ref, lambda_init)
        l1 = jnp.sum(l_ref[0], axis=-1, keepdims=True)
        l2 = jnp.sum(l_ref[1], axis=-1, keepdims=True)
        o_ref[...] = _diff_finish(acc_ref[0], l1, acc_ref[1], l2, lam, g_ref[...],
                                  lambda_init).astype(o_ref.dtype)


def _prompt_plan(S, tq_want=1024, tk_want=512, n_sub_want=2):
    tq = _pick_tile(S, tq_want)
    tk = _pick_tile(tq, tk_want)
    n_sub = _pick_tile(S // tk, n_sub_want)
    tks = tk * n_sub
    assert tq % CHUNK == 0 and tk % CHUNK == 0
    nq, nks = S // tq, S // tks

    def sub_kind(off):
        if off <= -tq:
            return None
        if off - (tk - 1) >= MAX_DISTANCE:
            return "far"
        return off

    pairs = [(a, b) for a in range(nq) for b in range(nks) if b * tks < (a + 1) * tq]
    pair_kinds = [tuple(sub_kind(a * tq - b * tks - sb * tk) for sb in range(n_sub))
                  for a, b in pairs]
    offs = sorted({o for kinds in pair_kinds for o in kinds if isinstance(o, int)})
    tile_of = {o: t for t, o in enumerate(offs)}

    def resolve(o):
        if o is None:
            return None
        if o == "far":
            return (-1, 0)
        return (tile_of[o], max(0, -o))

    step_kinds = sorted({tuple(resolve(o) for o in kinds) for kinds in pair_kinds},
                        key=lambda ks: repr(ks))
    kind_of = {ks: t for t, ks in enumerate(step_kinds)}
    qb = np.array([a for a, _ in pairs], np.int32)
    kb = np.array([b for _, b in pairs], np.int32)
    kind = np.array([kind_of[tuple(resolve(o) for o in kinds)] for kinds in pair_kinds], np.int32)
    first = np.array([int(b == 0) for _, b in pairs], np.int32)
    last = np.array([int(i + 1 == len(pairs) or pairs[i + 1][0] != a)
                     for i, (a, _) in enumerate(pairs)], np.int32)
    bases = tuple((max(o, 0), max(-o, 0)) for o in offs)
    return dict(tq=tq, tk=tk, tks=tks, nq=nq, nks=nks, bases=bases, step_kinds=tuple(step_kinds),
                tables=(qb, kb, kind, first, last))


def _diff_attention_prompt(qkv, tiles, cfar, lam_vecs, subln_g, plan, *, batch, heads, lambda_init,
                           name):
    T, D3 = qkv.shape
    D = D3 // 3
    dv = D // heads
    tq, tk, tks, nq, nks = (plan[k] for k in ("tq", "tk", "tks", "nq", "nks"))
    n_tiles = tiles.shape[1]
    n_steps = len(plan["tables"][0])

    def qmap(b, h, s, qb, kb, kind, first, last):
        return (b * nq + qb[s], h)

    def kmap(b, h, s, qb, kb, kind, first, last):
        return (b * nks + kb[s], heads + h)

    def vmap(b, h, s, qb, kb, kind, first, last):
        return (b * nks + kb[s], 2 * heads + h)

    vmem = (4 * _nbytes((tq, dv), BF16) + 8 * _nbytes((tks, dv), BF16)
            + 2 * _nbytes((n_tiles, tq, tk), F32) + 2 * _nbytes((tq, dv), F32)
            + 4 * _nbytes((tq, LANE), F32) + 10 * _nbytes((tq, tk), F32))
    grid_spec = pltpu.PrefetchScalarGridSpec(
        num_scalar_prefetch=5,
        grid=(batch, heads, n_steps),
        in_specs=[
            pl.BlockSpec((tq, dv), qmap),
            pl.BlockSpec((tks, dv), kmap),
            pl.BlockSpec((tks, dv), vmap),
            pl.BlockSpec((1, n_tiles, tq, tk), lambda b, h, s, *_: (h, 0, 0, 0)),
            pl.BlockSpec(memory_space=pltpu.SMEM),
            pl.BlockSpec(lam_vecs.shape, lambda b, h, s, *_: (0, 0)),
            pl.BlockSpec((1, dv), lambda b, h, s, *_: (0, 0)),
        ],
        out_specs=pl.BlockSpec((tq, dv), qmap),
        scratch_shapes=[pltpu.VMEM((2, tq, LANE), F32), pltpu.VMEM((2, tq, LANE), F32),
                        pltpu.VMEM((2, tq, dv), F32)],
    )
    return pl.pallas_call(
        functools.partial(_diff_prompt_body, lambda_init=lambda_init,
                          step_kinds=plan["step_kinds"], tk=tk),
        out_shape=jax.ShapeDtypeStruct((T, D), BF16),
        grid_spec=grid_spec,
        compiler_params=_cparams(("parallel", "parallel", "arbitrary"), vmem),
        name=name,
    )(*plan["tables"], qkv, qkv, qkv, tiles, cfar, lam_vecs, subln_g.reshape(1, dv))


def _diff_sample_body(q_ref, kn_ref, vn_ref, kc_ref, vc_ref, bc_ref, bn_ref, lam_ref, g_ref, o_ref,
                      m_ref, l_ref, acc_ref, *, lambda_init):
    c = pl.program_id(1)
    pc, heads, dv = kc_ref.shape
    dh = dv // 2
    nt = (((1,), (1,)), ((), ()))
    lane_tiles = [slice(i * LANE, (i + 1) * LANE) for i in range(pc // LANE)]

    @pl.when(c == 0)
    def _():
        m_ref[...] = jnp.full_like(m_ref, MASKED)
        l_ref[...] = jnp.zeros_like(l_ref)
        acc_ref[...] = jnp.zeros_like(acc_ref)

    kc = kc_ref[...].reshape(pc, heads * dv).astype(BF16)
    vc = vc_ref[...].reshape(pc, heads * dv).astype(BF16)
    for h in range(heads):
        v = vc[:, h * dv:(h + 1) * dv]
        for j in range(2):
            i = 2 * h + j
            cols = slice(h * dv + j * dh, h * dv + (j + 1) * dh)
            t = lax.dot_general(q_ref[:, cols], kc[:, cols], nt,
                                preferred_element_type=F32) + bc_ref[h]
            part = functools.reduce(jnp.maximum, [t[:, sl] for sl in lane_tiles])
            m_old = m_ref[i]
            m_new = jnp.maximum(m_old, jnp.max(part, axis=-1, keepdims=True))
            alpha = jnp.exp2(m_old - m_new)
            ps = [jnp.exp2(t[:, sl] - m_new) for sl in lane_tiles]
            l_ref[i] = alpha * l_ref[i] + functools.reduce(lambda a, b: a + b, ps)
            p = jnp.concatenate([x.astype(BF16) for x in ps], axis=1)
            alpha_v = jnp.concatenate([alpha] * (dv // LANE), axis=1)
            acc_ref[i] = alpha_v * acc_ref[i] + jnp.dot(p, v, preferred_element_type=F32)
            m_ref[i] = m_new

    @pl.when(c == pl.num_programs(1) - 1)
    def _():
        lam = _lambda(lam_ref, lambda_init)
        g = g_ref[...]
        for h in range(heads):
            vn = vn_ref[:, h * dv:(h + 1) * dv]
            accs, ls = [], []
            for j in range(2):
                i = 2 * h + j
                cols = slice(h * dv + j * dh, h * dv + (j + 1) * dh)
                t = lax.dot_general(q_ref[:, cols], kn_ref[:, cols], nt,
                                    preferred_element_type=F32) + bn_ref[h]
                m_old = m_ref[i][:, 0:1]
                m_new = jnp.maximum(m_old, jnp.max(t, axis=-1, keepdims=True))
                alpha = jnp.exp2(m_old - m_new)
                p = jnp.exp2(t - m_new)
                ls.append(alpha * jnp.sum(l_ref[i], axis=-1, keepdims=True)
                          + jnp.sum(p, axis=-1, keepdims=True))
                accs.append(alpha * acc_ref[i]
                            + jnp.dot(p.astype(BF16), vn, preferred_element_type=F32))
            o_ref[:, h * dv:(h + 1) * dv] = _diff_finish(
                accs[0], ls[0], accs[1], ls[1], lam, g, lambda_init).astype(o_ref.dtype)


def _diff_attention_sample(qkv, k_cache, v_cache, layer, bias_cache, bias_new, lam_vecs, subln_g, *,
                           lambda_init, pc_want=512, name):
    n_layers, batch, P, heads, dv = k_cache.shape
    D = heads * dv
    T = qkv.shape[0]
    n = T // batch
    pc = _pick_tile(P, pc_want)
    cache_spec = pl.BlockSpec((None, None, pc, heads, dv), lambda b, c: (layer, b, c, 0, 0))
    vmem = (6 * _nbytes((n, D), BF16) + 4 * _nbytes((pc, D), F32) + 4 * _nbytes((pc, D), F32)
            + 2 * _nbytes((heads, n, pc), F32) + 2 * _nbytes((n, D), BF16)
            + 2 * heads * (2 * _nbytes((n, LANE), F32) + _nbytes((n, dv), F32)))
    return pl.pallas_call(
        functools.partial(_diff_sample_body, lambda_init=lambda_init),
        out_shape=jax.ShapeDtypeStruct((T, D), BF16),
        grid=(batch, P // pc),
        in_specs=[
            pl.BlockSpec((n, D), lambda b, c: (b, 0)),
            pl.BlockSpec((n, D), lambda b, c: (b, 1)),
            pl.BlockSpec((n, D), lambda b, c: (b, 2)),
            cache_spec,
            cache_spec,
            pl.BlockSpec((heads, n, pc), lambda b, c: (0, 0, c)),
            pl.BlockSpec((heads, n, n), lambda b, c: (0, 0, 0)),
            pl.BlockSpec(lam_vecs.shape, lambda b, c: (0, 0)),
            pl.BlockSpec((1, dv), lambda b, c: (0, 0)),
        ],
        out_specs=pl.BlockSpec((n, D), lambda b, c: (b, 0)),
        scratch_shapes=[pltpu.VMEM((2 * heads, n, LANE), F32), pltpu.VMEM((2 * heads, n, LANE), F32),
                        pltpu.VMEM((2 * heads, n, dv), F32)],
        compiler_params=_cparams(("parallel", "arbitrary"), vmem),
        name=name,
    )(qkv, qkv, qkv, k_cache, v_cache, bias_cache, bias_new, lam_vecs, subln_g.reshape(1, dv))


def _rope_tables(pos, half):
    inv = ROPE_BASE ** (-jnp.arange(half, dtype=F32) / half)
    ang = pos.astype(F32)[:, None] * inv[None, :]
    return jnp.cos(ang), jnp.sin(ang)


def kernel(x_prompt, x_sample, state_ret, cache_k, cache_v, norm_g, ret_w_in, ret_w_out, ret_gn_g,
           diff_w_in, diff_w_out, diff_lambda, diff_subln_g, rel_bias, ffn_w_gu, ffn_w_down):
    B, S, D = x_prompt.shape
    DB, n_s, _ = x_sample.shape
    depth = norm_g.shape[0]
    ret_heads = state_ret.shape[2]
    diff_heads = cache_k.shape[3]
    past = cache_k.shape[2]

    pos_p = jnp.arange(S, dtype=jnp.int32)
    pos_s = past + jnp.arange(n_s, dtype=jnp.int32)
    half = D // ret_heads // 2
    rope_p = _rope_tables(pos_p, half)
    rope_s = _rope_tables(pos_s, half)
    lgs = jnp.log1p(-jnp.exp2(-5.0 - jnp.arange(ret_heads, dtype=F32)))
    bias_vec = _bias_vec(rel_bias)
    plan = _prompt_plan(S)
    tiles_p = _bias_tiles(bias_vec, plan["bases"], plan["tq"], plan["tk"], name="bias_tiles_prompt")
    bias_far = bias_vec[:, 0]
    bias_cache = _bias_tiles(bias_vec, [(past, 0)], n_s, past, name="bias_tiles_cache")[:, 0]
    bias_new = _bias_tiles(bias_vec, [(past, past)], n_s, n_s, name="bias_tiles_new")[:, 0]

    xp = x_prompt.reshape(B * S, D)
    xs = x_sample.reshape(DB * n_s, D)
    ret_p, ret_s, kp, vp, ks, vs = [], [], [], [], [], []
    for i in range(depth):
        g = norm_g[i]
        j = i // N_MIXERS
        if i % N_MIXERS == 0:
            w_in = ret_w_in[j].astype(BF16)
            w_out = ret_w_out[j].astype(BF16)
            qp = _norm_matmul(xp, g[0], w_in, name=f"ret_in_p{i}")
            qs = _norm_matmul(xs, g[0], w_in, name=f"ret_in_s{i}")
            mp, sp = _retention(qp, *rope_p, ret_gn_g[j], lgs, None, batch=B, heads=ret_heads,
                                name=f"retention_p{i}")
            ms, ss = _retention(qs, *rope_s, ret_gn_g[j], lgs, state_ret[j], batch=DB,
                                heads=ret_heads, name=f"retention_s{i}")
            ret_p.append(sp)
            ret_s.append(ss)
        else:
            lambda_init = 0.8 - 0.6 * math.exp(-0.3 * i)
            w_in = diff_w_in[j].astype(BF16)
            w_out = diff_w_out[j].astype(BF16)
            qkv_p, k1, v1 = _diff_in(xp, g[0], w_in, heads=diff_heads, name=f"diff_in_p{i}")
            qkv_s, k2, v2 = _diff_in(xs, g[0], w_in, heads=diff_heads, name=f"diff_in_s{i}")
            mp = _diff_attention_prompt(qkv_p, tiles_p, bias_far, diff_lambda[j], diff_subln_g[j],
                                        plan, batch=B, heads=diff_heads, lambda_init=lambda_init,
                                        name=f"diff_attn_p{i}")
            ms = _diff_attention_sample(qkv_s, cache_k, cache_v, j, bias_cache, bias_new,
                                        diff_lambda[j], diff_subln_g[j], lambda_init=lambda_init,
                                        name=f"diff_attn_s{i}")
            kp.append(k1)
            vp.append(v1)
            ks.append(k2)
            vs.append(v2)
        xp = _proj_norm_res(mp, w_out, xp, g[1], name=f"mix_out_p{i}")
        xs = _proj_norm_res(ms, w_out, xs, g[1], name=f"mix_out_s{i}")
        w_gu = ffn_w_gu[i].astype(BF16)
        w_down = ffn_w_down[i].astype(BF16)
        xp = _ffn(xp, g[2], w_gu, w_down, g[3], name=f"ffn_p{i}")
        xs = _ffn(xs, g[2], w_gu, w_down, g[3], name=f"ffn_s{i}")

    dvh = D // diff_heads
    return (xp.reshape(B, S, D), xs.reshape(DB, n_s, D),
            jnp.stack(ret_p), jnp.stack(ret_s),
            jnp.stack(kp).reshape(len(kp), B, S, diff_heads, dvh),
            jnp.stack(vp).reshape(len(vp), B, S, diff_heads, dvh),
            jnp.stack(ks).reshape(len(ks), DB, n_s, diff_heads, dvh),
            jnp.stack(vs).reshape(len(vs), DB, n_s, diff_heads, dvh))
```

```python
import functools
import math

import numpy as np
import jax
import jax.numpy as jnp
from jax import lax
from jax.experimental import pallas as pl
from jax.experimental.pallas import tpu as pltpu

F32 = jnp.float32
BF16 = jnp.bfloat16

NORM_EPS = 1e-6
ROPE_BASE = 10000.0
CHUNK = 64
N_BUCKETS = 32
MAX_DISTANCE = 128
N_MIXERS = 2
LOG2E = math.log2(math.e)
MASKED = -1e30

V7X_VMEM_BYTES = 64 * 1024 * 1024
VMEM_HEADROOM = 6 * 1024 * 1024
LANE = 128

RET_CHUNK = 256
BIAS_VEC_LEN = 512


def _cparams(semantics, vmem_estimate):
    limit = min(V7X_VMEM_BYTES - VMEM_HEADROOM, max(32 * 1024 * 1024, int(vmem_estimate * 1.25)))
    return pltpu.CompilerParams(dimension_semantics=semantics, vmem_limit_bytes=limit)


def _nbytes(shape, dtype):
    return int(np.prod(shape)) * jnp.dtype(dtype).itemsize


def _pick_tile(n, want):
    t = min(n, want)
    while n % t:
        t //= 2
    return t


def _rms(x, g):
    ms = jnp.mean(x * x, axis=-1, keepdims=True)
    return x * lax.rsqrt(ms + NORM_EPS) * g


def _norm_matmul_body(x_ref, g_ref, w_ref, o_ref, h_ref):
    @pl.when(pl.program_id(1) == 0)
    def _():
        h_ref[...] = _rms(x_ref[...], g_ref[...]).astype(BF16)

    o_ref[...] = jnp.dot(h_ref[...], w_ref[...], preferred_element_type=F32).astype(o_ref.dtype)


def _norm_matmul(x, g, w, *, tm_want=1024, tn_want=1024, name):
    T, D = x.shape
    N = w.shape[1]
    tm = _pick_tile(T, tm_want)
    tn = _pick_tile(N, tn_want)
    vmem = (2 * _nbytes((tm, D), F32) + _nbytes((tm, D), BF16) + 2 * _nbytes((D, tn), BF16)
            + 2 * _nbytes((tm, tn), BF16) + _nbytes((tm, tn), F32))
    return pl.pallas_call(
        _norm_matmul_body,
        out_shape=jax.ShapeDtypeStruct((T, N), BF16),
        grid=(T // tm, N // tn),
        in_specs=[
            pl.BlockSpec((tm, D), lambda i, j: (i, 0)),
            pl.BlockSpec((1, D), lambda i, j: (0, 0)),
            pl.BlockSpec((D, tn), lambda i, j: (0, j)),
        ],
        out_specs=pl.BlockSpec((tm, tn), lambda i, j: (i, j)),
        scratch_shapes=[pltpu.VMEM((tm, D), BF16)],
        compiler_params=_cparams(("parallel", "arbitrary"), vmem),
        name=name,
    )(x, g.reshape(1, D), w)


def _diff_in_body(x_ref, g_ref, w_ref, *rest, q_scale):
    qkv_ref, kf_ref, vf_ref, h_ref = rest[-4:]
    j = pl.program_id(1)

    def project():
        return jnp.dot(h_ref[...], w_ref[...], preferred_element_type=F32)

    @pl.when(j == 0)
    def _():
        h_ref[...] = _rms(x_ref[...], g_ref[...]).astype(BF16)
        qkv_ref[...] = (project() * q_scale).astype(BF16)

    @pl.when(j == 1)
    def _():
        y = project()
        qkv_ref[...] = y.astype(BF16)
        kf_ref[...] = y.reshape(kf_ref.shape)

    @pl.when(j == 2)
    def _():
        y = project()
        qkv_ref[...] = y.astype(BF16)
        vf_ref[...] = y.reshape(vf_ref.shape)


def _diff_in(x, g, w, rows_so_far, *, heads, n_layers, layer, tm_want=512, name):
    T, D = x.shape
    dv = D // heads
    tm = _pick_tile(T, tm_want)
    rows = jax.ShapeDtypeStruct((n_layers, T, heads, dv), F32)
    rows_spec = pl.BlockSpec((None, tm, heads, dv), lambda i, j: (layer, i, 0, 0))
    in_specs = [
        pl.BlockSpec((tm, D), lambda i, j: (i, 0)),
        pl.BlockSpec((1, D), lambda i, j: (0, 0)),
        pl.BlockSpec((D, D), lambda i, j: (0, j)),
    ]
    args = [x, g.reshape(1, D), w]
    aliases = {}
    if rows_so_far is not None:
        in_specs += [pl.BlockSpec(memory_space=pl.ANY)] * 2
        args += list(rows_so_far)
        aliases = {3: 1, 4: 2}
    vmem = (2 * _nbytes((tm, D), F32) + _nbytes((tm, D), BF16) + 2 * _nbytes((D, D), BF16)
            + 2 * _nbytes((tm, D), BF16) + 4 * _nbytes((tm, D), F32) + 2 * _nbytes((tm, D), F32))
    return pl.pallas_call(
        functools.partial(_diff_in_body, q_scale=(dv // 2) ** -0.5 * LOG2E),
        out_shape=[jax.ShapeDtypeStruct((T, 3 * D), BF16), rows, rows],
        grid=(T // tm, 3),
        in_specs=in_specs,
        out_specs=[pl.BlockSpec((tm, D), lambda i, j: (i, j)), rows_spec, rows_spec],
        scratch_shapes=[pltpu.VMEM((tm, D), BF16)],
        input_output_aliases=aliases,
        compiler_params=_cparams(("parallel", "arbitrary"), vmem),
        name=name,
    )(*args)


def _proj_norm_res_body(m_ref, w_ref, x_ref, g_ref, o_ref):
    y = jnp.dot(m_ref[...], w_ref[...], preferred_element_type=F32)
    o_ref[...] = x_ref[...] + _rms(y, g_ref[...])


def _proj_norm_res(m, w, x, g, *, tm_want=512, name):
    T, D = x.shape
    K = m.shape[1]
    tm = _pick_tile(T, tm_want)
    vmem = (2 * _nbytes((tm, K), BF16) + 2 * _nbytes((K, D), BF16) + 5 * _nbytes((tm, D), F32))
    return pl.pallas_call(
        _proj_norm_res_body,
        out_shape=jax.ShapeDtypeStruct((T, D), F32),
        grid=(T // tm,),
        in_specs=[
            pl.BlockSpec((tm, K), lambda i: (i, 0)),
            pl.BlockSpec((K, D), lambda i: (0, 0)),
            pl.BlockSpec((tm, D), lambda i: (i, 0)),
            pl.BlockSpec((1, D), lambda i: (0, 0)),
        ],
        out_specs=pl.BlockSpec((tm, D), lambda i: (i, 0)),
        compiler_params=_cparams(("parallel",), vmem),
        name=name,
    )(m, w, x, g.reshape(1, D))


def _ffn_body(x_ref, gpre_ref, wg_ref, wu_ref, wd_ref, gpost_ref, o_ref, h_ref, acc_ref):
    f = pl.program_id(1)

    @pl.when(f == 0)
    def _():
        h_ref[...] = _rms(x_ref[...], gpre_ref[...]).astype(BF16)
        acc_ref[...] = jnp.zeros_like(acc_ref)

    h = h_ref[...]
    gate = jnp.dot(h, wg_ref[...], preferred_element_type=F32)
    up = jnp.dot(h, wu_ref[...], preferred_element_type=F32)
    act = (gate * jax.nn.sigmoid(gate) * up).astype(BF16)
    acc_ref[...] += jnp.dot(act, wd_ref[...], preferred_element_type=F32)

    @pl.when(f == pl.num_programs(1) - 1)
    def _():
        o_ref[...] = x_ref[...] + _rms(acc_ref[...], gpost_ref[...])


def _ffn(x, g_pre, w_gu, w_down, g_post, *, tm_want=512, tf_want=512, name):
    T, D = x.shape
    F = w_down.shape[0]
    tm = _pick_tile(T, tm_want)
    tf = _pick_tile(F, tf_want)
    nf = F // tf
    vmem = (4 * _nbytes((tm, D), F32) + _nbytes((tm, D), BF16) + _nbytes((tm, D), F32)
            + 4 * _nbytes((D, tf), BF16) + 2 * _nbytes((tf, D), BF16) + 4 * _nbytes((tm, tf), F32))
    return pl.pallas_call(
        _ffn_body,
        out_shape=jax.ShapeDtypeStruct((T, D), F32),
        grid=(T // tm, nf),
        in_specs=[
            pl.BlockSpec((tm, D), lambda i, f: (i, 0)),
            pl.BlockSpec((1, D), lambda i, f: (0, 0)),
            pl.BlockSpec((D, tf), lambda i, f: (0, f)),
            pl.BlockSpec((D, tf), lambda i, f: (0, f + nf)),
            pl.BlockSpec((tf, D), lambda i, f: (f, 0)),
            pl.BlockSpec((1, D), lambda i, f: (0, 0)),
        ],
        out_specs=pl.BlockSpec((tm, D), lambda i, f: (i, 0)),
        scratch_shapes=[pltpu.VMEM((tm, D), BF16), pltpu.VMEM((tm, D), F32)],
        compiler_params=_cparams(("parallel", "arbitrary"), vmem),
        name=name,
    )(x, g_pre.reshape(1, D), w_gu, w_gu, w_down, g_post.reshape(1, D))


def _rotary(x, cos, sin):
    half = x.shape[-1] // 2
    x1, x2 = x[:, :half], x[:, half:]
    return jnp.concatenate([x1 * cos - x2 * sin, x1 * sin + x2 * cos], axis=-1)


def _retention_body(lg_ref, q_ref, k_ref, v_ref, g_ref, cos_ref, sin_ref, gn_ref, *rest,
                    has_state, k_scale):
    if has_state:
        s0_ref, o_ref, sout_ref, s_ref, dmat_ref, qdec_ref, kdec_ref = rest
    else:
        o_ref, sout_ref, s_ref, dmat_ref, qdec_ref, kdec_ref = rest
    hg = pl.program_id(1)
    c = pl.program_id(2)
    n_heads, dk, _ = s_ref.shape
    L = q_ref.shape[0]

    @pl.when(c == 0)
    def _():
        if has_state:
            s_ref[...] = s0_ref[0]
        else:
            s_ref[...] = jnp.zeros_like(s_ref)
        row = lax.broadcasted_iota(jnp.int32, (L, L), 0)
        col = lax.broadcasted_iota(jnp.int32, (L, L), 1)
        rel = (row - col).astype(F32)
        idx = lax.broadcasted_iota(jnp.int32, (L, dk), 0).astype(F32)
        for h in range(n_heads):
            lg = lg_ref[hg * n_heads + h]
            dmat_ref[h] = jnp.where(row >= col, jnp.exp(rel * lg), 0.0)
            qdec_ref[h] = jnp.exp((idx + 1.0) * lg)
            kdec_ref[h] = jnp.exp((L - 1.0 - idx) * lg)

    cos = cos_ref[...]
    sin = sin_ref[...]
    for h in range(n_heads):
        cols = slice(h * dk, (h + 1) * dk)
        lg = lg_ref[hg * n_heads + h]
        qr = _rotary(q_ref[:, cols].astype(F32), cos, sin)
        kr = _rotary(k_ref[:, cols].astype(F32), cos, sin) * k_scale
        v = v_ref[:, cols]
        qb = qr.astype(BF16)
        state = s_ref[h]

        scores = lax.dot_general(qb, kr.astype(BF16), (((1,), (1,)), ((), ())),
                                 preferred_element_type=F32) * dmat_ref[h]
        o = (jnp.dot(scores.astype(BF16), v, preferred_element_type=F32)
             + jnp.dot(qb, state.astype(BF16), preferred_element_type=F32) * qdec_ref[h])
        kd = (kr * kdec_ref[h]).astype(BF16)
        chunk_dec = jnp.exp(jnp.full((1, dk), L, F32) * lg)
        s_ref[h] = state * chunk_dec + lax.dot_general(kd, v, (((0,), (0,)), ((), ())),
                                                       preferred_element_type=F32)
        gate = g_ref[:, cols].astype(F32)
        o_ref[:, cols] = (gate * jax.nn.sigmoid(gate) * _rms(o, gn_ref[:, cols])).astype(o_ref.dtype)

    @pl.when(c == pl.num_programs(2) - 1)
    def _():
        sout_ref[0] = s_ref[...]


def _retention(qkvg, cos, sin, gn_g, lgs, state, *, batch, heads, heads_per_step=8, name):
    T, D4 = qkvg.shape
    D = D4 // 4
    dk = D // heads
    S = T // batch
    L = _pick_tile(S, RET_CHUNK)
    nc = S // L
    hps = _pick_tile(heads, heads_per_step)
    n_groups = heads // hps
    w = hps * dk
    has_state = state is not None

    def col(part):
        return lambda b, hg, c: (b * nc + c, part * n_groups + hg)

    state_spec = pl.BlockSpec((1, hps, dk, dk), lambda b, hg, c: (b, hg, 0, 0))
    in_specs = [
        pl.BlockSpec(memory_space=pltpu.SMEM),
        pl.BlockSpec((L, w), col(0)),
        pl.BlockSpec((L, w), col(1)),
        pl.BlockSpec((L, w), col(2)),
        pl.BlockSpec((L, w), col(3)),
        pl.BlockSpec((L, dk // 2), lambda b, hg, c: (c, 0)),
        pl.BlockSpec((L, dk // 2), lambda b, hg, c: (c, 0)),
        pl.BlockSpec((1, w), lambda b, hg, c: (0, hg)),
    ]
    args = [lgs, qkvg, qkvg, qkvg, qkvg, cos, sin, gn_g.reshape(1, D)]
    if has_state:
        in_specs.append(state_spec)
        args.append(state)
    vmem = (10 * _nbytes((L, w), BF16) + 4 * _nbytes((L, dk // 2), F32)
            + hps * (5 * _nbytes((dk, dk), F32) + _nbytes((L, L), F32) + 2 * _nbytes((L, dk), F32))
            + 16 * _nbytes((L, dk), F32))
    return pl.pallas_call(
        functools.partial(_retention_body, has_state=has_state, k_scale=dk ** -0.5),
        out_shape=[jax.ShapeDtypeStruct((T, D), BF16),
                   jax.ShapeDtypeStruct((batch, heads, dk, dk), F32)],
        grid=(batch, n_groups, nc),
        in_specs=in_specs,
        out_specs=[pl.BlockSpec((L, w), lambda b, hg, c: (b * nc + c, hg)), state_spec],
        scratch_shapes=[pltpu.VMEM((hps, dk, dk), F32), pltpu.VMEM((hps, L, L), F32),
                        pltpu.VMEM((hps, L, dk), F32), pltpu.VMEM((hps, L, dk), F32)],
        compiler_params=_cparams(("parallel", "parallel", "arbitrary"), vmem),
        name=name,
    )(*args)


def _t5_bucket(rel):
    nb = N_BUCKETS // 2
    max_exact = nb // 2
    bucket = jnp.where(rel > 0, nb, 0)
    n = jnp.abs(rel)
    large = max_exact + (jnp.log(jnp.maximum(n, 1).astype(F32) / max_exact)
                         / math.log(MAX_DISTANCE / max_exact) * (nb - max_exact)).astype(jnp.int32)
    large = jnp.minimum(large, nb - 1)
    return bucket + jnp.where(n < max_exact, n, large)


def _bias_vec_body(bucket_ref, rb_ref, o_ref):
    bkt = bucket_ref[...]
    n_buckets, heads = rb_ref.shape
    for h in range(heads):
        acc = jnp.zeros(bkt.shape, F32)
        for b in range(n_buckets):
            acc = acc + jnp.where(bkt == b, rb_ref[b, h], 0.0)
        o_ref[h:h + 1, :] = acc


def _bias_vec(rel_bias):
    heads = rel_bias.shape[1]
    rel = jnp.arange(BIAS_VEC_LEN, dtype=jnp.int32) - (BIAS_VEC_LEN - CHUNK)
    bucket = _t5_bucket(rel).reshape(1, BIAS_VEC_LEN)
    return pl.pallas_call(
        _bias_vec_body,
        out_shape=jax.ShapeDtypeStruct((heads, BIAS_VEC_LEN), F32),
        in_specs=[pl.BlockSpec(memory_space=pltpu.VMEM), pl.BlockSpec(memory_space=pltpu.SMEM)],
        out_specs=pl.BlockSpec(memory_space=pltpu.VMEM),
        name="t5_bias_lookup",
    )(bucket, rel_bias.astype(F32))


def _bias_tiles_body(vec_ref, o_ref, *, bases):
    _, _, tq, tk = o_ref.shape
    base, width = _bias_window(tq, tk)
    r0 = BIAS_VEC_LEN - CHUNK
    vec = vec_ref[0]
    first = vec[:, 0:1]
    last = vec[:, BIAS_VEC_LEN - 1:BIAS_VEC_LEN]
    padded = vec
    if width > BIAS_VEC_LEN:
        padded = jnp.concatenate([vec, jnp.broadcast_to(last, (1, width - BIAS_VEC_LEN))], axis=1)
    slot = lax.broadcasted_iota(jnp.int32, (1, width), 1)
    row = lax.broadcasted_iota(jnp.int32, (tq, tk), 0)
    col = lax.broadcasted_iota(jnp.int32, (tq, tk), 1)
    for t, (q0, k0) in enumerate(bases):
        start = r0 - base - (q0 - k0)
        idx = slot + start
        window = pltpu.roll(padded, (-start) % width, 1)
        window = jnp.where(idx < 0, first, jnp.where(idx >= BIAS_VEC_LEN, last, window))
        skew = pltpu.roll(jnp.broadcast_to(window, (tq, width)), 0, 1, stride=1, stride_axis=0)
        visible = (col + k0) // CHUNK <= (row + q0) // CHUNK
        o_ref[0, t] = jnp.where(visible, skew[:, base:base + tk] * LOG2E, MASKED)


def _bias_window(tq, tk):
    base = -(-tq // LANE) * LANE
    return base, max(-(-(base + tk) // LANE) * LANE, BIAS_VEC_LEN)


def _bias_tiles(vec, bases, tq, tk, *, name):
    heads = vec.shape[0]
    n = len(bases)
    _, width = _bias_window(tq, tk)
    vmem = 2 * _nbytes((n, tq, tk), F32) + 3 * _nbytes((tq, width), F32) + 4 * _nbytes((tq, tk), F32)
    return pl.pallas_call(
        functools.partial(_bias_tiles_body, bases=tuple(bases)),
        out_shape=jax.ShapeDtypeStruct((heads, n, tq, tk), F32),
        grid=(heads,),
        in_specs=[pl.BlockSpec((1, 1, BIAS_VEC_LEN), lambda h: (h, 0, 0))],
        out_specs=pl.BlockSpec((1, n, tq, tk), lambda h: (h, 0, 0, 0)),
        compiler_params=_cparams(("parallel",), vmem),
        name=name,
    )(vec.reshape(heads, 1, BIAS_VEC_LEN))


def _lambda(lam_ref, lambda_init):
    lv = lam_ref[...]
    a = jnp.sum(lv[0:1] * lv[1:2], axis=-1, keepdims=True)
    b = jnp.sum(lv[2:3] * lv[3:4], axis=-1, keepdims=True)
    return jnp.exp(a) - jnp.exp(b) + lambda_init


def _diff_finish(acc1, l1, acc2, l2, lam, g, lambda_init):
    o = acc1 * (1.0 / l1) - lam * (acc2 * (1.0 / l2))
    return _rms(o, g) * (1.0 - lambda_init)


def _diff_prompt_body(qb_ref, kb_ref, kind_ref, first_ref, last_ref,
                      q_ref, k_ref, v_ref, bias_ref, cfar_ref, lam_ref, g_ref, o_ref,
                      m_ref, l_ref, acc_ref, *, lambda_init, step_kinds, tk):
    h = pl.program_id(1)
    s = pl.program_id(2)
    dh = q_ref.shape[1] // 2
    kind = kind_ref[s]

    @pl.when(first_ref[s] == 1)
    def _():
        m_ref[...] = jnp.full_like(m_ref, MASKED)
        l_ref[...] = jnp.zeros_like(l_ref)
        acc_ref[...] = jnp.zeros_like(acc_ref)

    lane_tiles = [slice(c * LANE, (c + 1) * LANE) for c in range(tk // LANE)]

    def update(sub, tile, r0):
        keys = slice(sub * tk, (sub + 1) * tk)
        rows = slice(r0, q_ref.shape[0])
        v = v_ref[keys, :]
        shift = cfar_ref[h] * LOG2E if tile < 0 else 0.0
        for j in range(2):
            q = q_ref[rows, j * dh:(j + 1) * dh]
            k = k_ref[keys, j * dh:(j + 1) * dh]
            t = lax.dot_general(q, k, (((1,), (1,)), ((), ())), preferred_element_type=F32)
            if tile >= 0:
                t = t + bias_ref[0, tile, rows, :]
            part = t[:, lane_tiles[0]]
            for sl in lane_tiles[1:]:
                part = jnp.maximum(part, t[:, sl])
            m_old = m_ref[j, rows, :]
            m_new = jnp.maximum(m_old, jnp.max(part, axis=-1, keepdims=True) + shift)
            m_sub = m_new - shift
            alpha = jnp.exp2(m_old - m_new)
            ps = [jnp.exp2(t[:, sl] - m_sub) for sl in lane_tiles]
            l_ref[j, rows, :] = alpha * l_ref[j, rows, :] + functools.reduce(lambda a, b: a + b, ps)
            p = jnp.concatenate([x.astype(BF16) for x in ps], axis=1)
            alpha_v = jnp.concatenate([alpha] * (v.shape[1] // LANE), axis=1)
            acc_ref[j, rows, :] = (alpha_v * acc_ref[j, rows, :]
                                   + jnp.dot(p, v, preferred_element_type=F32))
            m_ref[j, rows, :] = m_new

    for t_id, kinds in enumerate(step_kinds):
        @pl.when(kind == t_id)
        def _(kinds=kinds):
            for sub, sub_kind in enumerate(kinds):
                if sub_kind is not None:
                    update(sub, *sub_kind)

    @pl.when(last_ref[s] == 1)
    def _():
        lam = _lambda(lam_ref, lambda_init)
        l1 = jnp.sum(l_ref[0], axis=-1, keepdims=True)
        l2 = jnp.sum(l_ref[1], axis=-1, keepdims=True)
        o_ref[...] = _diff_finish(acc_ref[0], l1, acc_ref[1], l2, lam, g_ref[...],
                                  lambda_init).astype(o_ref.dtype)


def _prompt_plan(S, tq_want=1024, tk_want=512, n_sub_want=4):
    tq = _pick_tile(S, tq_want)
    tk = _pick_tile(tq, tk_want)
    n_sub = _pick_tile(S // tk, n_sub_want)
    tks = tk * n_sub
    assert tq % CHUNK == 0 and tk % CHUNK == 0
    nq, nks = S // tq, S // tks

    def sub_kind(off):
        if off <= -tq:
            return None
        if off - (tk - 1) >= MAX_DISTANCE:
            return "far"
        return off

    pairs = [(a, b) for a in range(nq) for b in range(nks) if b * tks < (a + 1) * tq]
    pair_kinds = [tuple(sub_kind(a * tq - b * tks - sb * tk) for sb in range(n_sub))
                  for a, b in pairs]
    offs = sorted({o for kinds in pair_kinds for o in kinds if isinstance(o, int)})
    tile_of = {o: t for t, o in enumerate(offs)}

    def resolve(o):
        if o is None:
            return None
        if o == "far":
            return (-1, 0)
        return (tile_of[o], max(0, -o))

    step_kinds = sorted({tuple(resolve(o) for o in kinds) for kinds in pair_kinds},
                        key=lambda ks: repr(ks))
    kind_of = {ks: t for t, ks in enumerate(step_kinds)}
    qb = np.array([a for a, _ in pairs], np.int32)
    kb = np.array([b for _, b in pairs], np.int32)
    kind = np.array([kind_of[tuple(resolve(o) for o in kinds)] for kinds in pair_kinds], np.int32)
    first = np.array([int(b == 0) for _, b in pairs], np.int32)
    last = np.array([int(i + 1 == len(pairs) or pairs[i + 1][0] != a)
                     for i, (a, _) in enumerate(pairs)], np.int32)
    bases = tuple((max(o, 0), max(-o, 0)) for o in offs)
    return dict(tq=tq, tk=tk, tks=tks, nq=nq, nks=nks, bases=bases, step_kinds=tuple(step_kinds),
                tables=(qb, kb, kind, first, last))


def _diff_attention_prompt(qkv, tiles, cfar, lam_vecs, subln_g, plan, *, batch, heads, lambda_init,
                           name):
    T, D3 = qkv.shape
    D = D3 // 3
    dv = D // heads
    tq, tk, tks, nq, nks = (plan[k] for k in ("tq", "tk", "tks", "nq", "nks"))
    n_tiles = tiles.shape[1]
    n_steps = len(plan["tables"][0])

    def qmap(b, h, s, qb, kb, kind, first, last):
        return (b * nq + qb[s], h)

    def kmap(b, h, s, qb, kb, kind, first, last):
        return (b * nks + kb[s], heads + h)

    def vmap(b, h, s, qb, kb, kind, first, last):
        return (b * nks + kb[s], 2 * heads + h)

    vmem = (4 * _nbytes((tq, dv), BF16) + 8 * _nbytes((tks, dv), BF16)
            + 2 * _nbytes((n_tiles, tq, tk), F32) + 2 * _nbytes((tq, dv), F32)
            + 4 * _nbytes((tq, LANE), F32) + 10 * _nbytes((tq, tk), F32))
    grid_spec = pltpu.PrefetchScalarGridSpec(
        num_scalar_prefetch=5,
        grid=(batch, heads, n_steps),
        in_specs=[
            pl.BlockSpec((tq, dv), qmap),
            pl.BlockSpec((tks, dv), kmap),
            pl.BlockSpec((tks, dv), vmap),
            pl.BlockSpec((1, n_tiles, tq, tk), lambda b, h, s, *_: (h, 0, 0, 0)),
            pl.BlockSpec(memory_space=pltpu.SMEM),
            pl.BlockSpec(lam_vecs.shape, lambda b, h, s, *_: (0, 0)),
            pl.BlockSpec((1, dv), lambda b, h, s, *_: (0, 0)),
        ],
        out_specs=pl.BlockSpec((tq, dv), qmap),
        scratch_shapes=[pltpu.VMEM((2, tq, LANE), F32), pltpu.VMEM((2, tq, LANE), F32),
                        pltpu.VMEM((2, tq, dv), F32)],
    )
    return pl.pallas_call(
        functools.partial(_diff_prompt_body, lambda_init=lambda_init,
                          step_kinds=plan["step_kinds"], tk=tk),
        out_shape=jax.ShapeDtypeStruct((T, D), BF16),
        grid_spec=grid_spec,
        compiler_params=_cparams(("parallel", "parallel", "arbitrary"), vmem),
        name=name,
    )(*plan["tables"], qkv, qkv, qkv, tiles, cfar, lam_vecs, subln_g.reshape(1, dv))


def _diff_sample_body(q_ref, kn_ref, vn_ref, kc_ref, vc_ref, bc_ref, bn_ref, lam_ref, g_ref, o_ref,
                      m_ref, l_ref, acc_ref, *, lambda_init):
    c = pl.program_id(1)
    pc, heads, dv = kc_ref.shape
    dh = dv // 2
    nt = (((1,), (1,)), ((), ()))
    lane_tiles = [slice(i * LANE, (i + 1) * LANE) for i in range(pc // LANE)]

    @pl.when(c == 0)
    def _():
        m_ref[...] = jnp.full_like(m_ref, MASKED)
        l_ref[...] = jnp.zeros_like(l_ref)
        acc_ref[...] = jnp.zeros_like(acc_ref)

    kc = kc_ref[...].reshape(pc, heads * dv).astype(BF16)
    vc = vc_ref[...].reshape(pc, heads * dv).astype(BF16)
    for h in range(heads):
        v = vc[:, h * dv:(h + 1) * dv]
        for j in range(2):
            i = 2 * h + j
            cols = slice(h * dv + j * dh, h * dv + (j + 1) * dh)
            t = lax.dot_general(q_ref[:, cols], kc[:, cols], nt,
                                preferred_element_type=F32) + bc_ref[h]
            part = functools.reduce(jnp.maximum, [t[:, sl] for sl in lane_tiles])
            m_old = m_ref[i]
            m_new = jnp.maximum(m_old, jnp.max(part, axis=-1, keepdims=True))
            alpha = jnp.exp2(m_old - m_new)
            ps = [jnp.exp2(t[:, sl] - m_new) for sl in lane_tiles]
            l_ref[i] = alpha * l_ref[i] + functools.reduce(lambda a, b: a + b, ps)
            p = jnp.concatenate([x.astype(BF16) for x in ps], axis=1)
            alpha_v = jnp.concatenate([alpha] * (dv // LANE), axis=1)
            acc_ref[i] = alpha_v * acc_ref[i] + jnp.dot(p, v, preferred_element_type=F32)
            m_ref[i] = m_new

    @pl.when(c == pl.num_programs(1) - 1)
    def _():
        lam = _lambda(lam_ref, lambda_init)
        g = g_ref[...]
        for h in range(heads):
            vn = vn_ref[:, h * dv:(h + 1) * dv]
            accs, ls = [], []
            for j in range(2):
                i = 2 * h + j
                cols = slice(h * dv + j * dh, h * dv + (j + 1) * dh)
                t = lax.dot_general(q_ref[:, cols], kn_ref[:, cols], nt,
                                    preferred_element_type=F32) + bn_ref[h]
                m_old = m_ref[i][:, 0:1]
                m_new = jnp.maximum(m_old, jnp.max(t, axis=-1, keepdims=True))
                alpha = jnp.exp2(m_old - m_new)
                p = jnp.exp2(t - m_new)
                ls.append(alpha * jnp.sum(l_ref[i], axis=-1, keepdims=True)
                          + jnp.sum(p, axis=-1, keepdims=True))
                accs.append(alpha * acc_ref[i]
                            + jnp.dot(p.astype(BF16), vn, preferred_element_type=F32))
            o_ref[:, h * dv:(h + 1) * dv] = _diff_finish(
                accs[0], ls[0], accs[1], ls[1], lam, g, lambda_init).astype(o_ref.dtype)


def _diff_attention_sample(qkv, k_cache, v_cache, layer, bias_cache, bias_new, lam_vecs, subln_g, *,
                           lambda_init, pc_want=512, name):
    n_layers, batch, P, heads, dv = k_cache.shape
    D = heads * dv
    T = qkv.shape[0]
    n = T // batch
    pc = _pick_tile(P, pc_want)
    cache_spec = pl.BlockSpec((None, None, pc, heads, dv), lambda b, c: (layer, b, c, 0, 0))
    vmem = (6 * _nbytes((n, D), BF16) + 4 * _nbytes((pc, D), F32) + 4 * _nbytes((pc, D), F32)
            + 2 * _nbytes((heads, n, pc), F32) + 2 * _nbytes((n, D), BF16)
            + 2 * heads * (2 * _nbytes((n, LANE), F32) + _nbytes((n, dv), F32)))
    return pl.pallas_call(
        functools.partial(_diff_sample_body, lambda_init=lambda_init),
        out_shape=jax.ShapeDtypeStruct((T, D), BF16),
        grid=(batch, P // pc),
        in_specs=[
            pl.BlockSpec((n, D), lambda b, c: (b, 0)),
            pl.BlockSpec((n, D), lambda b, c: (b, 1)),
            pl.BlockSpec((n, D), lambda b, c: (b, 2)),
            cache_spec,
            cache_spec,
            pl.BlockSpec((heads, n, pc), lambda b, c: (0, 0, c)),
            pl.BlockSpec((heads, n, n), lambda b, c: (0, 0, 0)),
            pl.BlockSpec(lam_vecs.shape, lambda b, c: (0, 0)),
            pl.BlockSpec((1, dv), lambda b, c: (0, 0)),
        ],
        out_specs=pl.BlockSpec((n, D), lambda b, c: (b, 0)),
        scratch_shapes=[pltpu.VMEM((2 * heads, n, LANE), F32), pltpu.VMEM((2 * heads, n, LANE), F32),
                        pltpu.VMEM((2 * heads, n, dv), F32)],
        compiler_params=_cparams(("parallel", "arbitrary"), vmem),
        name=name,
    )(qkv, qkv, qkv, k_cache, v_cache, bias_cache, bias_new, lam_vecs, subln_g.reshape(1, dv))


def _rope_tables(pos, half):
    inv = ROPE_BASE ** (-jnp.arange(half, dtype=F32) / half)
    ang = pos.astype(F32)[:, None] * inv[None, :]
    return jnp.cos(ang), jnp.sin(ang)


def kernel(x_prompt, x_sample, state_ret, cache_k, cache_v, norm_g, ret_w_in, ret_w_out, ret_gn_g,
           diff_w_in, diff_w_out, diff_lambda, diff_subln_g, rel_bias, ffn_w_gu, ffn_w_down):
    B, S, D = x_prompt.shape
    DB, n_s, _ = x_sample.shape
    depth = norm_g.shape[0]
    ret_heads = state_ret.shape[2]
    diff_heads = cache_k.shape[3]
    past = cache_k.shape[2]

    pos_p = jnp.arange(S, dtype=jnp.int32)
    pos_s = past + jnp.arange(n_s, dtype=jnp.int32)
    half = D // ret_heads // 2
    rope_p = _rope_tables(pos_p, half)
    rope_s = _rope_tables(pos_s, half)
    lgs = jnp.log1p(-jnp.exp2(-5.0 - jnp.arange(ret_heads, dtype=F32)))
    bias_vec = _bias_vec(rel_bias)
    plan = _prompt_plan(S)
    tiles_p = _bias_tiles(bias_vec, plan["bases"], plan["tq"], plan["tk"], name="bias_tiles_prompt")
    bias_far = bias_vec[:, 0]
    bias_cache = _bias_tiles(bias_vec, [(past, 0)], n_s, past, name="bias_tiles_cache")[:, 0]
    bias_new = _bias_tiles(bias_vec, [(past, past)], n_s, n_s, name="bias_tiles_new")[:, 0]

    xp = x_prompt.reshape(B * S, D)
    xs = x_sample.reshape(DB * n_s, D)
    ret_p, ret_s = [], []
    rows_p = rows_s = None
    n_diff = depth // N_MIXERS
    for i in range(depth):
        g = norm_g[i]
        j = i // N_MIXERS
        if i % N_MIXERS == 0:
            w_in = ret_w_in[j].astype(BF16)
            w_out = ret_w_out[j].astype(BF16)
            qp = _norm_matmul(xp, g[0], w_in, name=f"ret_in_p{i}")
            qs = _norm_matmul(xs, g[0], w_in, name=f"ret_in_s{i}")
            mp, sp = _retention(qp, *rope_p, ret_gn_g[j], lgs, None, batch=B, heads=ret_heads,
                                name=f"retention_p{i}")
            ms, ss = _retention(qs, *rope_s, ret_gn_g[j], lgs, state_ret[j], batch=DB,
                                heads=ret_heads, name=f"retention_s{i}")
            ret_p.append(sp)
            ret_s.append(ss)
        else:
            lambda_init = 0.8 - 0.6 * math.exp(-0.3 * i)
            w_in = diff_w_in[j].astype(BF16)
            w_out = diff_w_out[j].astype(BF16)
            qkv_p, *rows_p = _diff_in(xp, g[0], w_in, rows_p, heads=diff_heads, n_layers=n_diff,
                                      layer=j, name=f"diff_in_p{i}")
            qkv_s, *rows_s = _diff_in(xs, g[0], w_in, rows_s, heads=diff_heads, n_layers=n_diff,
                                      layer=j, name=f"diff_in_s{i}")
            mp = _diff_attention_prompt(qkv_p, tiles_p, bias_far, diff_lambda[j], diff_subln_g[j],
                                        plan, batch=B, heads=diff_heads, lambda_init=lambda_init,
                                        name=f"diff_attn_p{i}")
            ms = _diff_attention_sample(qkv_s, cache_k, cache_v, j, bias_cache, bias_new,
                                        diff_lambda[j], diff_subln_g[j], lambda_init=lambda_init,
                                        name=f"diff_attn_s{i}")
        xp = _proj_norm_res(mp, w_out, xp, g[1], name=f"mix_out_p{i}")
        xs = _proj_norm_res(ms, w_out, xs, g[1], name=f"mix_out_s{i}")
        w_gu = ffn_w_gu[i].astype(BF16)
        w_down = ffn_w_down[i].astype(BF16)
        xp = _ffn(xp, g[2], w_gu, w_down, g[3], name=f"ffn_p{i}")
        xs = _ffn(xs, g[2], w_gu, w_down, g[3], name=f"ffn_s{i}")

    dvh = D // diff_heads
    return (xp.reshape(B, S, D), xs.reshape(DB, n_s, D),
            jnp.stack(ret_p), jnp.stack(ret_s),
            rows_p[0].reshape(n_diff, B, S, diff_heads, dvh),
            rows_p[1].reshape(n_diff, B, S, diff_heads, dvh),
            rows_s[0].reshape(n_diff, DB, n_s, diff_heads, dvh),
            rows_s[1].reshape(n_diff, DB, n_s, diff_heads, dvh))
```

```python
import functools
import math

import numpy as np
import jax
import jax.numpy as jnp
from jax import lax
from jax.experimental import pallas as pl
from jax.experimental.pallas import tpu as pltpu

F32 = jnp.float32
BF16 = jnp.bfloat16

NORM_EPS = 1e-6
ROPE_BASE = 10000.0
CHUNK = 64
N_BUCKETS = 32
MAX_DISTANCE = 128
N_MIXERS = 2
LOG2E = math.log2(math.e)
MASKED = -1e30

V7X_VMEM_BYTES = 64 * 1024 * 1024
VMEM_HEADROOM = 6 * 1024 * 1024
LANE = 128

RET_CHUNK = 256
BIAS_VEC_LEN = 512


def _cparams(semantics, vmem_estimate):
    limit = min(V7X_VMEM_BYTES - VMEM_HEADROOM, max(32 * 1024 * 1024, int(vmem_estimate * 1.25)))
    return pltpu.CompilerParams(dimension_semantics=semantics, vmem_limit_bytes=limit)


def _nbytes(shape, dtype):
    return int(np.prod(shape)) * jnp.dtype(dtype).itemsize


def _pick_tile(n, want):
    t = min(n, want)
    while n % t:
        t //= 2
    return t


def _rms(x, g):
    ms = jnp.mean(x * x, axis=-1, keepdims=True)
    return x * lax.rsqrt(ms + NORM_EPS) * g


def _norm_matmul_body(x_ref, g_ref, w_ref, o_ref, h_ref):
    @pl.when(pl.program_id(1) == 0)
    def _():
        h_ref[...] = _rms(x_ref[...], g_ref[...]).astype(BF16)

    o_ref[...] = jnp.dot(h_ref[...], w_ref[...], preferred_element_type=F32).astype(o_ref.dtype)


def _norm_matmul(x, g, w, *, tm_want=1024, tn_want=1024, name):
    T, D = x.shape
    N = w.shape[1]
    tm = _pick_tile(T, tm_want)
    tn = _pick_tile(N, tn_want)
    vmem = (2 * _nbytes((tm, D), F32) + _nbytes((tm, D), BF16) + 2 * _nbytes((D, tn), BF16)
            + 2 * _nbytes((tm, tn), BF16) + _nbytes((tm, tn), F32))
    return pl.pallas_call(
        _norm_matmul_body,
        out_shape=jax.ShapeDtypeStruct((T, N), BF16),
        grid=(T // tm, N // tn),
        in_specs=[
            pl.BlockSpec((tm, D), lambda i, j: (i, 0)),
            pl.BlockSpec((1, D), lambda i, j: (0, 0)),
            pl.BlockSpec((D, tn), lambda i, j: (0, j)),
        ],
        out_specs=pl.BlockSpec((tm, tn), lambda i, j: (i, j)),
        scratch_shapes=[pltpu.VMEM((tm, D), BF16)],
        compiler_params=_cparams(("parallel", "arbitrary"), vmem),
        name=name,
    )(x, g.reshape(1, D), w)


def _diff_in_body(h_ref, w_ref, *rest, q_scale):
    qkv_ref, kf_ref, vf_ref = rest[-3:]
    part = pl.program_id(0)

    def project():
        return jnp.dot(h_ref[...], w_ref[...], preferred_element_type=F32)

    @pl.when(part == 0)
    def _():
        qkv_ref[...] = (project() * q_scale).astype(BF16)

    @pl.when(part == 1)
    def _():
        y = project()
        qkv_ref[...] = y.astype(BF16)
        kf_ref[...] = y.reshape(kf_ref.shape)

    @pl.when(part == 2)
    def _():
        y = project()
        qkv_ref[...] = y.astype(BF16)
        vf_ref[...] = y.reshape(vf_ref.shape)


def _diff_in(h, w, rows_so_far, *, heads, n_layers, layer, tm_want=512, name):
    T, D = h.shape
    dv = D // heads
    tm = _pick_tile(T, tm_want)
    n_tiles = T // tm
    rows = jax.ShapeDtypeStruct((n_layers, T, heads, dv), F32)

    def rows_spec(part):
        def index(j, i):
            return (layer, jnp.where(j == part, i, jnp.where(j < part, 0, n_tiles - 1)), 0, 0)
        return pl.BlockSpec((None, tm, heads, dv), index)

    in_specs = [
        pl.BlockSpec((tm, D), lambda j, i: (i, 0)),
        pl.BlockSpec((D, D), lambda j, i: (0, j)),
    ]
    args = [h, w]
    aliases = {}
    if rows_so_far is not None:
        in_specs += [pl.BlockSpec(memory_space=pl.ANY)] * 2
        args += list(rows_so_far)
        aliases = {2: 1, 3: 2}
    vmem = (2 * _nbytes((tm, D), BF16) + 2 * _nbytes((D, D), BF16)
            + 2 * _nbytes((tm, D), BF16) + 4 * _nbytes((tm, D), F32) + 2 * _nbytes((tm, D), F32))
    return pl.pallas_call(
        functools.partial(_diff_in_body, q_scale=(dv // 2) ** -0.5 * LOG2E),
        out_shape=[jax.ShapeDtypeStruct((T, 3 * D), BF16), rows, rows],
        grid=(3, n_tiles),
        in_specs=in_specs,
        out_specs=[pl.BlockSpec((tm, D), lambda j, i: (i, j)), rows_spec(1), rows_spec(2)],
        input_output_aliases=aliases,
        compiler_params=_cparams(("arbitrary", "arbitrary"), vmem),
        name=name,
    )(*args)


def _proj_norm_res_body(m_ref, w_ref, x_ref, g_ref, o_ref):
    y = jnp.dot(m_ref[...], w_ref[...], preferred_element_type=F32)
    o_ref[...] = x_ref[...] + _rms(y, g_ref[...])


def _proj_norm_res(m, w, x, g, *, tm_want=512, name):
    T, D = x.shape
    K = m.shape[1]
    tm = _pick_tile(T, tm_want)
    vmem = (2 * _nbytes((tm, K), BF16) + 2 * _nbytes((K, D), BF16) + 5 * _nbytes((tm, D), F32))
    return pl.pallas_call(
        _proj_norm_res_body,
        out_shape=jax.ShapeDtypeStruct((T, D), F32),
        grid=(T // tm,),
        in_specs=[
            pl.BlockSpec((tm, K), lambda i: (i, 0)),
            pl.BlockSpec((K, D), lambda i: (0, 0)),
            pl.BlockSpec((tm, D), lambda i: (i, 0)),
            pl.BlockSpec((1, D), lambda i: (0, 0)),
        ],
        out_specs=pl.BlockSpec((tm, D), lambda i: (i, 0)),
        compiler_params=_cparams(("parallel",), vmem),
        name=name,
    )(m, w, x, g.reshape(1, D))


def _ffn_body(x_ref, gpre_ref, wg_ref, wu_ref, wd_ref, gpost_ref, *rest):
    with_next = len(rest) == 5
    o_ref = rest[1] if with_next else rest[0]
    h_ref, acc_ref = rest[-2:]
    f = pl.program_id(1)

    @pl.when(f == 0)
    def _():
        h_ref[...] = _rms(x_ref[...], gpre_ref[...]).astype(BF16)
        acc_ref[...] = jnp.zeros_like(acc_ref)

    h = h_ref[...]
    gate = jnp.dot(h, wg_ref[...], preferred_element_type=F32)
    up = jnp.dot(h, wu_ref[...], preferred_element_type=F32)
    act = (gate * jax.nn.sigmoid(gate) * up).astype(BF16)
    acc_ref[...] += jnp.dot(act, wd_ref[...], preferred_element_type=F32)

    @pl.when(f == pl.num_programs(1) - 1)
    def _():
        out = x_ref[...] + _rms(acc_ref[...], gpost_ref[...])
        o_ref[...] = out
        if with_next:
            rest[2][...] = _rms(out, rest[0][...]).astype(BF16)


def _ffn(x, g_pre, w_gu, w_down, g_post, g_next=None, *, tm_want=512, tf_want=512, name):
    T, D = x.shape
    F = w_down.shape[0]
    tm = _pick_tile(T, tm_want)
    tf = _pick_tile(F, tf_want)
    nf = F // tf
    row = pl.BlockSpec((tm, D), lambda i, f: (i, 0))
    vec = pl.BlockSpec((1, D), lambda i, f: (0, 0))
    in_specs = [row, vec,
                pl.BlockSpec((D, tf), lambda i, f: (0, f)),
                pl.BlockSpec((D, tf), lambda i, f: (0, f + nf)),
                pl.BlockSpec((tf, D), lambda i, f: (f, 0)),
                vec]
    args = [x, g_pre.reshape(1, D), w_gu, w_gu, w_down, g_post.reshape(1, D)]
    out_shape = [jax.ShapeDtypeStruct((T, D), F32)]
    out_specs = [row]
    if g_next is not None:
        in_specs.append(vec)
        args.append(g_next.reshape(1, D))
        out_shape.append(jax.ShapeDtypeStruct((T, D), BF16))
        out_specs.append(row)
    vmem = (4 * _nbytes((tm, D), F32) + 3 * _nbytes((tm, D), BF16) + _nbytes((tm, D), F32)
            + 4 * _nbytes((D, tf), BF16) + 2 * _nbytes((tf, D), BF16) + 4 * _nbytes((tm, tf), F32))
    outs = pl.pallas_call(
        _ffn_body,
        out_shape=out_shape,
        grid=(T // tm, nf),
        in_specs=in_specs,
        out_specs=out_specs,
        scratch_shapes=[pltpu.VMEM((tm, D), BF16), pltpu.VMEM((tm, D), F32)],
        compiler_params=_cparams(("parallel", "arbitrary"), vmem),
        name=name,
    )(*args)
    return outs if g_next is not None else outs[0]


def _rotary(x, cos, sin):
    half = x.shape[-1] // 2
    x1, x2 = x[:, :half], x[:, half:]
    return jnp.concatenate([x1 * cos - x2 * sin, x1 * sin + x2 * cos], axis=-1)


def _retention_body(lg_ref, q_ref, k_ref, v_ref, g_ref, cos_ref, sin_ref, gn_ref, *rest,
                    has_state, k_scale):
    if has_state:
        s0_ref, o_ref, sout_ref, s_ref, dmat_ref, qdec_ref, kdec_ref = rest
    else:
        o_ref, sout_ref, s_ref, dmat_ref, qdec_ref, kdec_ref = rest
    hg = pl.program_id(1)
    c = pl.program_id(2)
    n_heads, dk, _ = s_ref.shape
    L = q_ref.shape[0]

    @pl.when(c == 0)
    def _():
        if has_state:
            s_ref[...] = s0_ref[0]
        else:
            s_ref[...] = jnp.zeros_like(s_ref)
        row = lax.broadcasted_iota(jnp.int32, (L, L), 0)
        col = lax.broadcasted_iota(jnp.int32, (L, L), 1)
        rel = (row - col).astype(F32)
        idx = lax.broadcasted_iota(jnp.int32, (L, dk), 0).astype(F32)
        for h in range(n_heads):
            lg = lg_ref[hg * n_heads + h]
            dmat_ref[h] = jnp.where(row >= col, jnp.exp(rel * lg), 0.0)
            qdec_ref[h] = jnp.exp((idx + 1.0) * lg)
            kdec_ref[h] = jnp.exp((L - 1.0 - idx) * lg)

    cos = cos_ref[...]
    sin = sin_ref[...]
    for h in range(n_heads):
        cols = slice(h * dk, (h + 1) * dk)
        lg = lg_ref[hg * n_heads + h]
        qr = _rotary(q_ref[:, cols].astype(F32), cos, sin)
        kr = _rotary(k_ref[:, cols].astype(F32), cos, sin) * k_scale
        v = v_ref[:, cols]
        qb = qr.astype(BF16)
        state = s_ref[h]

        scores = lax.dot_general(qb, kr.astype(BF16), (((1,), (1,)), ((), ())),
                                 preferred_element_type=F32) * dmat_ref[h]
        o = (jnp.dot(scores.astype(BF16), v, preferred_element_type=F32)
             + jnp.dot(qb, state.astype(BF16), preferred_element_type=F32) * qdec_ref[h])
        kd = (kr * kdec_ref[h]).astype(BF16)
        chunk_dec = jnp.exp(jnp.full((1, dk), L, F32) * lg)
        s_ref[h] = state * chunk_dec + lax.dot_general(kd, v, (((0,), (0,)), ((), ())),
                                                       preferred_element_type=F32)
        gate = g_ref[:, cols].astype(F32)
        o_ref[:, cols] = (gate * jax.nn.sigmoid(gate) * _rms(o, gn_ref[:, cols])).astype(o_ref.dtype)

    @pl.when(c == pl.num_programs(2) - 1)
    def _():
        sout_ref[0] = s_ref[...]


def _retention(qkvg, cos, sin, gn_g, lgs, state, *, batch, heads, heads_per_step=8, name):
    T, D4 = qkvg.shape
    D = D4 // 4
    dk = D // heads
    S = T // batch
    L = _pick_tile(S, RET_CHUNK)
    nc = S // L
    hps = _pick_tile(heads, heads_per_step)
    n_groups = heads // hps
    w = hps * dk
    has_state = state is not None

    def col(part):
        return lambda b, hg, c: (b * nc + c, part * n_groups + hg)

    state_spec = pl.BlockSpec((1, hps, dk, dk), lambda b, hg, c: (b, hg, 0, 0))
    in_specs = [
        pl.BlockSpec(memory_space=pltpu.SMEM),
        pl.BlockSpec((L, w), col(0)),
        pl.BlockSpec((L, w), col(1)),
        pl.BlockSpec((L, w), col(2)),
        pl.BlockSpec((L, w), col(3)),
        pl.BlockSpec((L, dk // 2), lambda b, hg, c: (c, 0)),
        pl.BlockSpec((L, dk // 2), lambda b, hg, c: (c, 0)),
        pl.BlockSpec((1, w), lambda b, hg, c: (0, hg)),
    ]
    args = [lgs, qkvg, qkvg, qkvg, qkvg, cos, sin, gn_g.reshape(1, D)]
    if has_state:
        in_specs.append(state_spec)
        args.append(state)
    vmem = (10 * _nbytes((L, w), BF16) + 4 * _nbytes((L, dk // 2), F32)
            + hps * (5 * _nbytes((dk, dk), F32) + _nbytes((L, L), F32) + 2 * _nbytes((L, dk), F32))
            + 16 * _nbytes((L, dk), F32))
    return pl.pallas_call(
        functools.partial(_retention_body, has_state=has_state, k_scale=dk ** -0.5),
        out_shape=[jax.ShapeDtypeStruct((T, D), BF16),
                   jax.ShapeDtypeStruct((batch, heads, dk, dk), F32)],
        grid=(batch, n_groups, nc),
        in_specs=in_specs,
        out_specs=[pl.BlockSpec((L, w), lambda b, hg, c: (b * nc + c, hg)), state_spec],
        scratch_shapes=[pltpu.VMEM((hps, dk, dk), F32), pltpu.VMEM((hps, L, L), F32),
                        pltpu.VMEM((hps, L, dk), F32), pltpu.VMEM((hps, L, dk), F32)],
        compiler_params=_cparams(("parallel", "parallel", "arbitrary"), vmem),
        name=name,
    )(*args)


def _t5_bucket(rel):
    nb = N_BUCKETS // 2
    max_exact = nb // 2
    bucket = jnp.where(rel > 0, nb, 0)
    n = jnp.abs(rel)
    large = max_exact + (jnp.log(jnp.maximum(n, 1).astype(F32) / max_exact)
                         / math.log(MAX_DISTANCE / max_exact) * (nb - max_exact)).astype(jnp.int32)
    large = jnp.minimum(large, nb - 1)
    return bucket + jnp.where(n < max_exact, n, large)


def _bias_vec_body(bucket_ref, rb_ref, o_ref):
    bkt = bucket_ref[...]
    n_buckets, heads = rb_ref.shape
    for h in range(heads):
        acc = jnp.zeros(bkt.shape, F32)
        for b in range(n_buckets):
            acc = acc + jnp.where(bkt == b, rb_ref[b, h], 0.0)
        o_ref[h:h + 1, :] = acc


def _bias_vec(rel_bias):
    heads = rel_bias.shape[1]
    rel = jnp.arange(BIAS_VEC_LEN, dtype=jnp.int32) - (BIAS_VEC_LEN - CHUNK)
    bucket = _t5_bucket(rel).reshape(1, BIAS_VEC_LEN)
    return pl.pallas_call(
        _bias_vec_body,
        out_shape=jax.ShapeDtypeStruct((heads, BIAS_VEC_LEN), F32),
        in_specs=[pl.BlockSpec(memory_space=pltpu.VMEM), pl.BlockSpec(memory_space=pltpu.SMEM)],
        out_specs=pl.BlockSpec(memory_space=pltpu.VMEM),
        name="t5_bias_lookup",
    )(bucket, rel_bias.astype(F32))


def _bias_tiles_body(vec_ref, o_ref, *, bases):
    _, _, tq, tk = o_ref.shape
    base, width = _bias_window(tq, tk)
    r0 = BIAS_VEC_LEN - CHUNK
    vec = vec_ref[0]
    first = vec[:, 0:1]
    last = vec[:, BIAS_VEC_LEN - 1:BIAS_VEC_LEN]
    padded = vec
    if width > BIAS_VEC_LEN:
        padded = jnp.concatenate([vec, jnp.broadcast_to(last, (1, width - BIAS_VEC_LEN))], axis=1)
    slot = lax.broadcasted_iota(jnp.int32, (1, width), 1)
    row = lax.broadcasted_iota(jnp.int32, (tq, tk), 0)
    col = lax.broadcasted_iota(jnp.int32, (tq, tk), 1)
    for t, (q0, k0) in enumerate(bases):
        start = r0 - base - (q0 - k0)
        idx = slot + start
        window = pltpu.roll(padded, (-start) % width, 1)
        window = jnp.where(idx < 0, first, jnp.where(idx >= BIAS_VEC_LEN, last, window))
        skew = pltpu.roll(jnp.broadcast_to(window, (tq, width)), 0, 1, stride=1, stride_axis=0)
        visible = (col + k0) // CHUNK <= (row + q0) // CHUNK
        o_ref[0, t] = jnp.where(visible, skew[:, base:base + tk] * LOG2E, MASKED)


def _bias_window(tq, tk):
    base = -(-tq // LANE) * LANE
    return base, max(-(-(base + tk) // LANE) * LANE, BIAS_VEC_LEN)


def _bias_tiles(vec, bases, tq, tk, *, name):
    heads = vec.shape[0]
    n = len(bases)
    _, width = _bias_window(tq, tk)
    vmem = 2 * _nbytes((n, tq, tk), F32) + 3 * _nbytes((tq, width), F32) + 4 * _nbytes((tq, tk), F32)
    return pl.pallas_call(
        functools.partial(_bias_tiles_body, bases=tuple(bases)),
        out_shape=jax.ShapeDtypeStruct((heads, n, tq, tk), F32),
        grid=(heads,),
        in_specs=[pl.BlockSpec((1, 1, BIAS_VEC_LEN), lambda h: (h, 0, 0))],
        out_specs=pl.BlockSpec((1, n, tq, tk), lambda h: (h, 0, 0, 0)),
        compiler_params=_cparams(("parallel",), vmem),
        name=name,
    )(vec.reshape(heads, 1, BIAS_VEC_LEN))


def _lambda(lam_ref, lambda_init):
    lv = lam_ref[...]
    a = jnp.sum(lv[0:1] * lv[1:2], axis=-1, keepdims=True)
    b = jnp.sum(lv[2:3] * lv[3:4], axis=-1, keepdims=True)
    return jnp.exp(a) - jnp.exp(b) + lambda_init


def _diff_finish(acc1, l1, acc2, l2, lam, g, lambda_init):
    o = acc1 * (1.0 / l1) - lam * (acc2 * (1.0 / l2))
    return _rms(o, g) * (1.0 - lambda_init)


def _diff_prompt_body(qb_ref, kb_ref, kind_ref, first_ref, last_ref,
                      q_ref, k_ref, v_ref, bias_ref, cfar_ref, lam_ref, g_ref, o_ref,
                      m_ref, l_ref, acc_ref, *, lambda_init, step_kinds, tk):
    h = pl.program_id(1)
    s = pl.program_id(2)
    dh = q_ref.shape[1] // 2
    kind = kind_ref[s]

    @pl.when(first_ref[s] == 1)
    def _():
        m_ref[...] = jnp.full_like(m_ref, MASKED)
        l_ref[...] = jnp.zeros_like(l_ref)
        acc_ref[...] = jnp.zeros_like(acc_ref)

    lane_tiles = [slice(c * LANE, (c + 1) * LANE) for c in range(tk // LANE)]

    def update(sub, tile, r0):
        keys = slice(sub * tk, (sub + 1) * tk)
        rows = slice(r0, q_ref.shape[0])
        v = v_ref[keys, :]
        shift = cfar_ref[h] * LOG2E if tile < 0 else 0.0
        for j in range(2):
            q = q_ref[rows, j * dh:(j + 1) * dh]
            k = k_ref[keys, j * dh:(j + 1) * dh]
            t = lax.dot_general(q, k, (((1,), (1,)), ((), ())), preferred_element_type=F32)
            if tile >= 0:
                t = t + bias_ref[0, tile, rows, :]
            part = t[:, lane_tiles[0]]
            for sl in lane_tiles[1:]:
                part = jnp.maximum(part, t[:, sl])
            m_old = m_ref[j, rows, :]
            m_new = jnp.maximum(m_old, jnp.max(part, axis=-1, keepdims=True) + shift)
            m_sub = m_new - shift
            alpha = jnp.exp2(m_old - m_new)
            ps = [jnp.exp2(t[:, sl] - m_sub) for sl in lane_tiles]
            l_ref[j, rows, :] = alpha * l_ref[j, rows, :] + functools.reduce(lambda a, b: a + b, ps)
            p = jnp.concatenate([x.astype(BF16) for x in ps], axis=1)
            alpha_v = jnp.concatenate([alpha] * (v.shape[1] // LANE), axis=1)
            acc_ref[j, rows, :] = (alpha_v * acc_ref[j, rows, :]
                                   + jnp.dot(p, v, preferred_element_type=F32))
            m_ref[j, rows, :] = m_new

    for t_id, kinds in enumerate(step_kinds):
        @pl.when(kind == t_id)
        def _(kinds=kinds):
            for sub, sub_kind in enumerate(kinds):
                if sub_kind is not None:
                    update(sub, *sub_kind)

    @pl.when(last_ref[s] == 1)
    def _():
        lam = _lambda(lam_ref, lambda_init)
        l1 = jnp.sum(l_ref[0], axis=-1, keepdims=True)
        l2 = jnp.sum(l_ref[1], axis=-1, keepdims=True)
        o_ref[...] = _diff_finish(acc_ref[0], l1, acc_ref[1], l2, lam, g_ref[...],
                                  lambda_init).astype(o_ref.dtype)


def _prompt_plan(S, tq_want=1024, tk_want=512, n_sub_want=4):
    tq = _pick_tile(S, tq_want)
    tk = _pick_tile(tq, tk_want)
    n_sub = _pick_tile(S // tk, n_sub_want)
    tks = tk * n_sub
    assert tq % CHUNK == 0 and tk % CHUNK == 0
    nq, nks = S // tq, S // tks

    def sub_kind(off):
        if off <= -tq:
            return None
        if off - (tk - 1) >= MAX_DISTANCE:
            return "far"
        return off

    pairs = [(a, b) for a in range(nq) for b in range(nks) if b * tks < (a + 1) * tq]
    pair_kinds = [tuple(sub_kind(a * tq - b * tks - sb * tk) for sb in range(n_sub))
                  for a, b in pairs]
    offs = sorted({o for kinds in pair_kinds for o in kinds if isinstance(o, int)})
    tile_of = {o: t for t, o in enumerate(offs)}

    def resolve(o):
        if o is None:
            return None
        if o == "far":
            return (-1, 0)
        return (tile_of[o], max(0, -o))

    step_kinds = sorted({tuple(resolve(o) for o in kinds) for kinds in pair_kinds},
                        key=lambda ks: repr(ks))
    kind_of = {ks: t for t, ks in enumerate(step_kinds)}
    qb = np.array([a for a, _ in pairs], np.int32)
    kb = np.array([b for _, b in pairs], np.int32)
    kind = np.array([kind_of[tuple(resolve(o) for o in kinds)] for kinds in pair_kinds], np.int32)
    first = np.array([int(b == 0) for _, b in pairs], np.int32)
    last = np.array([int(i + 1 == len(pairs) or pairs[i + 1][0] != a)
                     for i, (a, _) in enumerate(pairs)], np.int32)
    bases = tuple((max(o, 0), max(-o, 0)) for o in offs)
    return dict(tq=tq, tk=tk, tks=tks, nq=nq, nks=nks, bases=bases, step_kinds=tuple(step_kinds),
                tables=(qb, kb, kind, first, last))


def _diff_attention_prompt(qkv, tiles, cfar, lam_vecs, subln_g, plan, *, batch, heads, lambda_init,
                           name):
    T, D3 = qkv.shape
    D = D3 // 3
    dv = D // heads
    tq, tk, tks, nq, nks = (plan[k] for k in ("tq", "tk", "tks", "nq", "nks"))
    n_tiles = tiles.shape[1]
    n_steps = len(plan["tables"][0])

    def qmap(b, h, s, qb, kb, kind, first, last):
        return (b * nq + qb[s], h)

    def kmap(b, h, s, qb, kb, kind, first, last):
        return (b * nks + kb[s], heads + h)

    def vmap(b, h, s, qb, kb, kind, first, last):
        return (b * nks + kb[s], 2 * heads + h)

    vmem = (4 * _nbytes((tq, dv), BF16) + 8 * _nbytes((tks, dv), BF16)
            + 2 * _nbytes((n_tiles, tq, tk), F32) + 2 * _nbytes((tq, dv), F32)
            + 4 * _nbytes((tq, LANE), F32) + 10 * _nbytes((tq, tk), F32))
    grid_spec = pltpu.PrefetchScalarGridSpec(
        num_scalar_prefetch=5,
        grid=(batch, heads, n_steps),
        in_specs=[
            pl.BlockSpec((tq, dv), qmap),
            pl.BlockSpec((tks, dv), kmap),
            pl.BlockSpec((tks, dv), vmap),
            pl.BlockSpec((1, n_tiles, tq, tk), lambda b, h, s, *_: (h, 0, 0, 0)),
            pl.BlockSpec(memory_space=pltpu.SMEM),
            pl.BlockSpec(lam_vecs.shape, lambda b, h, s, *_: (0, 0)),
            pl.BlockSpec((1, dv), lambda b, h, s, *_: (0, 0)),
        ],
        out_specs=pl.BlockSpec((tq, dv), qmap),
        scratch_shapes=[pltpu.VMEM((2, tq, LANE), F32), pltpu.VMEM((2, tq, LANE), F32),
                        pltpu.VMEM((2, tq, dv), F32)],
    )
    return pl.pallas_call(
        functools.partial(_diff_prompt_body, lambda_init=lambda_init,
                          step_kinds=plan["step_kinds"], tk=tk),
        out_shape=jax.ShapeDtypeStruct((T, D), BF16),
        grid_spec=grid_spec,
        compiler_params=_cparams(("parallel", "parallel", "arbitrary"), vmem),
        name=name,
    )(*plan["tables"], qkv, qkv, qkv, tiles, cfar, lam_vecs, subln_g.reshape(1, dv))


def _diff_sample_body(q_ref, kn_ref, vn_ref, kc_ref, vc_ref, bc_ref, bn_ref, lam_ref, g_ref, o_ref,
                      m_ref, l_ref, acc_ref, *, lambda_init):
    c = pl.program_id(1)
    pc, heads, dv = kc_ref.shape
    dh = dv // 2
    nt = (((1,), (1,)), ((), ()))
    lane_tiles = [slice(i * LANE, (i + 1) * LANE) for i in range(pc // LANE)]

    @pl.when(c == 0)
    def _():
        m_ref[...] = jnp.full_like(m_ref, MASKED)
        l_ref[...] = jnp.zeros_like(l_ref)
        acc_ref[...] = jnp.zeros_like(acc_ref)

    kc = kc_ref[...].reshape(pc, heads * dv).astype(BF16)
    vc = vc_ref[...].reshape(pc, heads * dv).astype(BF16)
    for h in range(heads):
        v = vc[:, h * dv:(h + 1) * dv]
        for j in range(2):
            i = 2 * h + j
            cols = slice(h * dv + j * dh, h * dv + (j + 1) * dh)
            t = lax.dot_general(q_ref[:, cols], kc[:, cols], nt,
                                preferred_element_type=F32) + bc_ref[h]
            part = functools.reduce(jnp.maximum, [t[:, sl] for sl in lane_tiles])
            m_old = m_ref[i]
            m_new = jnp.maximum(m_old, jnp.max(part, axis=-1, keepdims=True))
            alpha = jnp.exp2(m_old - m_new)
            ps = [jnp.exp2(t[:, sl] - m_new) for sl in lane_tiles]
            l_ref[i] = alpha * l_ref[i] + functools.reduce(lambda a, b: a + b, ps)
            p = jnp.concatenate([x.astype(BF16) for x in ps], axis=1)
            alpha_v = jnp.concatenate([alpha] * (dv // LANE), axis=1)
            acc_ref[i] = alpha_v * acc_ref[i] + jnp.dot(p, v, preferred_element_type=F32)
            m_ref[i] = m_new

    @pl.when(c == pl.num_programs(1) - 1)
    def _():
        lam = _lambda(lam_ref, lambda_init)
        g = g_ref[...]
        for h in range(heads):
            vn = vn_ref[:, h * dv:(h + 1) * dv]
            accs, ls = [], []
            for j in range(2):
                i = 2 * h + j
                cols = slice(h * dv + j * dh, h * dv + (j + 1) * dh)
                t = lax.dot_general(q_ref[:, cols], kn_ref[:, cols], nt,
                                    preferred_element_type=F32) + bn_ref[h]
                m_old = m_ref[i][:, 0:1]
                m_new = jnp.maximum(m_old, jnp.max(t, axis=-1, keepdims=True))
                alpha = jnp.exp2(m_old - m_new)
                p = jnp.exp2(t - m_new)
                ls.append(alpha * jnp.sum(l_ref[i], axis=-1, keepdims=True)
                          + jnp.sum(p, axis=-1, keepdims=True))
                accs.append(alpha * acc_ref[i]
                            + jnp.dot(p.astype(BF16), vn, preferred_element_type=F32))
            o_ref[:, h * dv:(h + 1) * dv] = _diff_finish(
                accs[0], ls[0], accs[1], ls[1], lam, g, lambda_init).astype(o_ref.dtype)


def _diff_attention_sample(qkv, k_cache, v_cache, layer, bias_cache, bias_new, lam_vecs, subln_g, *,
                           lambda_init, pc_want=512, name):
    n_layers, batch, P, heads, dv = k_cache.shape
    D = heads * dv
    T = qkv.shape[0]
    n = T // batch
    pc = _pick_tile(P, pc_want)
    cache_spec = pl.BlockSpec((None, None, pc, heads, dv), lambda b, c: (layer, b, c, 0, 0))
    vmem = (6 * _nbytes((n, D), BF16) + 4 * _nbytes((pc, D), F32) + 4 * _nbytes((pc, D), F32)
            + 2 * _nbytes((heads, n, pc), F32) + 2 * _nbytes((n, D), BF16)
            + 2 * heads * (2 * _nbytes((n, LANE), F32) + _nbytes((n, dv), F32)))
    return pl.pallas_call(
        functools.partial(_diff_sample_body, lambda_init=lambda_init),
        out_shape=jax.ShapeDtypeStruct((T, D), BF16),
        grid=(batch, P // pc),
        in_specs=[
            pl.BlockSpec((n, D), lambda b, c: (b, 0)),
            pl.BlockSpec((n, D), lambda b, c: (b, 1)),
            pl.BlockSpec((n, D), lambda b, c: (b, 2)),
            cache_spec,
            cache_spec,
            pl.BlockSpec((heads, n, pc), lambda b, c: (0, 0, c)),
            pl.BlockSpec((heads, n, n), lambda b, c: (0, 0, 0)),
            pl.BlockSpec(lam_vecs.shape, lambda b, c: (0, 0)),
            pl.BlockSpec((1, dv), lambda b, c: (0, 0)),
        ],
        out_specs=pl.BlockSpec((n, D), lambda b, c: (b, 0)),
        scratch_shapes=[pltpu.VMEM((2 * heads, n, LANE), F32), pltpu.VMEM((2 * heads, n, LANE), F32),
                        pltpu.VMEM((2 * heads, n, dv), F32)],
        compiler_params=_cparams(("parallel", "arbitrary"), vmem),
        name=name,
    )(qkv, qkv, qkv, k_cache, v_cache, bias_cache, bias_new, lam_vecs, subln_g.reshape(1, dv))


def _rope_tables(pos, half):
    inv = ROPE_BASE ** (-jnp.arange(half, dtype=F32) / half)
    ang = pos.astype(F32)[:, None] * inv[None, :]
    return jnp.cos(ang), jnp.sin(ang)


def kernel(x_prompt, x_sample, state_ret, cache_k, cache_v, norm_g, ret_w_in, ret_w_out, ret_gn_g,
           diff_w_in, diff_w_out, diff_lambda, diff_subln_g, rel_bias, ffn_w_gu, ffn_w_down):
    B, S, D = x_prompt.shape
    DB, n_s, _ = x_sample.shape
    depth = norm_g.shape[0]
    ret_heads = state_ret.shape[2]
    diff_heads = cache_k.shape[3]
    past = cache_k.shape[2]

    pos_p = jnp.arange(S, dtype=jnp.int32)
    pos_s = past + jnp.arange(n_s, dtype=jnp.int32)
    half = D // ret_heads // 2
    rope_p = _rope_tables(pos_p, half)
    rope_s = _rope_tables(pos_s, half)
    lgs = jnp.log1p(-jnp.exp2(-5.0 - jnp.arange(ret_heads, dtype=F32)))
    bias_vec = _bias_vec(rel_bias)
    plan = _prompt_plan(S)
    tiles_p = _bias_tiles(bias_vec, plan["bases"], plan["tq"], plan["tk"], name="bias_tiles_prompt")
    bias_far = bias_vec[:, 0]
    bias_cache = _bias_tiles(bias_vec, [(past, 0)], n_s, past, name="bias_tiles_cache")[:, 0]
    bias_new = _bias_tiles(bias_vec, [(past, past)], n_s, n_s, name="bias_tiles_new")[:, 0]

    xp = x_prompt.reshape(B * S, D)
    xs = x_sample.reshape(DB * n_s, D)
    ret_p, ret_s = [], []
    rows_p = rows_s = None
    n_diff = depth // N_MIXERS
    for i in range(depth):
        g = norm_g[i]
        j = i // N_MIXERS
        if i % N_MIXERS == 0:
            w_in = ret_w_in[j].astype(BF16)
            w_out = ret_w_out[j].astype(BF16)
            qp = _norm_matmul(xp, g[0], w_in, name=f"ret_in_p{i}")
            qs = _norm_matmul(xs, g[0], w_in, name=f"ret_in_s{i}")
            mp, sp = _retention(qp, *rope_p, ret_gn_g[j], lgs, None, batch=B, heads=ret_heads,
                                name=f"retention_p{i}")
            ms, ss = _retention(qs, *rope_s, ret_gn_g[j], lgs, state_ret[j], batch=DB,
                                heads=ret_heads, name=f"retention_s{i}")
            ret_p.append(sp)
            ret_s.append(ss)
        else:
            lambda_init = 0.8 - 0.6 * math.exp(-0.3 * i)
            w_in = diff_w_in[j].astype(BF16)
            w_out = diff_w_out[j].astype(BF16)
            qkv_p, *rows_p = _diff_in(hp, w_in, rows_p, heads=diff_heads, n_layers=n_diff,
                                      layer=j, name=f"diff_in_p{i}")
            qkv_s, *rows_s = _diff_in(hs, w_in, rows_s, heads=diff_heads, n_layers=n_diff,
                                      layer=j, name=f"diff_in_s{i}")
            mp = _diff_attention_prompt(qkv_p, tiles_p, bias_far, diff_lambda[j], diff_subln_g[j],
                                        plan, batch=B, heads=diff_heads, lambda_init=lambda_init,
                                        name=f"diff_attn_p{i}")
            ms = _diff_attention_sample(qkv_s, cache_k, cache_v, j, bias_cache, bias_new,
                                        diff_lambda[j], diff_subln_g[j], lambda_init=lambda_init,
                                        name=f"diff_attn_s{i}")
        xp = _proj_norm_res(mp, w_out, xp, g[1], name=f"mix_out_p{i}")
        xs = _proj_norm_res(ms, w_out, xs, g[1], name=f"mix_out_s{i}")
        w_gu = ffn_w_gu[i].astype(BF16)
        w_down = ffn_w_down[i].astype(BF16)
        if i + 1 < depth and (i + 1) % N_MIXERS != 0:
            xp, hp = _ffn(xp, g[2], w_gu, w_down, g[3], norm_g[i + 1, 0], name=f"ffn_p{i}")
            xs, hs = _ffn(xs, g[2], w_gu, w_down, g[3], norm_g[i + 1, 0], name=f"ffn_s{i}")
        else:
            xp = _ffn(xp, g[2], w_gu, w_down, g[3], name=f"ffn_p{i}")
            xs = _ffn(xs, g[2], w_gu, w_down, g[3], name=f"ffn_s{i}")

    dvh = D // diff_heads
    return (xp.reshape(B, S, D), xs.reshape(DB, n_s, D),
            jnp.stack(ret_p), jnp.stack(ret_s),
            rows_p[0].reshape(n_diff, B, S, diff_heads, dvh),
            rows_p[1].reshape(n_diff, B, S, diff_heads, dvh),
            rows_s[0].reshape(n_diff, DB, n_s, diff_heads, dvh),
            rows_s[1].reshape(n_diff, DB, n_s, diff_heads, dvh))
```

```python
import functools
import math

import numpy as np
import jax
import jax.numpy as jnp
from jax import lax
from jax.experimental import pallas as pl
from jax.experimental.pallas import tpu as pltpu

F32 = jnp.float32
BF16 = jnp.bfloat16

NORM_EPS = 1e-6
ROPE_BASE = 10000.0
CHUNK = 64
N_BUCKETS = 32
MAX_DISTANCE = 128
N_MIXERS = 2
LOG2E = math.log2(math.e)
MASKED = -1e30

V7X_VMEM_BYTES = 64 * 1024 * 1024
VMEM_HEADROOM = 6 * 1024 * 1024
LANE = 128

RET_CHUNK = 256
BIAS_VEC_LEN = 512


def _cparams(semantics, vmem_estimate):
    limit = min(V7X_VMEM_BYTES - VMEM_HEADROOM, max(32 * 1024 * 1024, int(vmem_estimate * 1.25)))
    return pltpu.CompilerParams(dimension_semantics=semantics, vmem_limit_bytes=limit)


def _nbytes(shape, dtype):
    return int(np.prod(shape)) * jnp.dtype(dtype).itemsize


def _pick_tile(n, want):
    t = min(n, want)
    while n % t:
        t //= 2
    return t


def _rms(x, g):
    ms = jnp.mean(x * x, axis=-1, keepdims=True)
    return x * lax.rsqrt(ms + NORM_EPS) * g


def _norm_matmul_body(x_ref, g_ref, w_ref, o_ref, h_ref):
    @pl.when(pl.program_id(1) == 0)
    def _():
        h_ref[...] = _rms(x_ref[...], g_ref[...]).astype(BF16)

    o_ref[...] = jnp.dot(h_ref[...], w_ref[...], preferred_element_type=F32).astype(o_ref.dtype)


def _norm_matmul(x, g, w, layer, *, tm_want=1024, tn_want=1024, name):
    T, D = x.shape
    N = w.shape[2]
    tm = _pick_tile(T, tm_want)
    tn = _pick_tile(N, tn_want)
    vmem = (2 * _nbytes((tm, D), F32) + _nbytes((tm, D), BF16) + 2 * _nbytes((D, tn), BF16)
            + 2 * _nbytes((tm, tn), BF16) + _nbytes((tm, tn), F32))
    return pl.pallas_call(
        _norm_matmul_body,
        out_shape=jax.ShapeDtypeStruct((T, N), BF16),
        grid=(T // tm, N // tn),
        in_specs=[
            pl.BlockSpec((tm, D), lambda i, j: (i, 0)),
            pl.BlockSpec((1, D), lambda i, j: (0, 0)),
            pl.BlockSpec((None, D, tn), lambda i, j: (layer, 0, j)),
        ],
        out_specs=pl.BlockSpec((tm, tn), lambda i, j: (i, j)),
        scratch_shapes=[pltpu.VMEM((tm, D), BF16)],
        compiler_params=_cparams(("parallel", "arbitrary"), vmem),
        name=name,
    )(x, g.reshape(1, D), w)


def _diff_in_body(h_ref, w_ref, *rest, q_scale):
    qkv_ref, kf_ref, vf_ref = rest[-3:]
    part = pl.program_id(0)

    def project():
        return jnp.dot(h_ref[...], w_ref[...], preferred_element_type=F32)

    @pl.when(part == 0)
    def _():
        qkv_ref[...] = (project() * q_scale).astype(BF16)

    @pl.when(part == 1)
    def _():
        y = project()
        qkv_ref[...] = y.astype(BF16)
        kf_ref[...] = y.reshape(kf_ref.shape)

    @pl.when(part == 2)
    def _():
        y = project()
        qkv_ref[...] = y.astype(BF16)
        vf_ref[...] = y.reshape(vf_ref.shape)


def _diff_in(h, w, rows_so_far, *, heads, n_layers, layer, tm_want=512, name):
    T, D = h.shape
    dv = D // heads
    tm = _pick_tile(T, tm_want)
    n_tiles = T // tm
    rows = jax.ShapeDtypeStruct((n_layers, T, heads, dv), F32)

    def rows_spec(part):
        def index(j, i):
            return (layer, jnp.where(j == part, i, jnp.where(j < part, 0, n_tiles - 1)), 0, 0)
        return pl.BlockSpec((None, tm, heads, dv), index)

    in_specs = [
        pl.BlockSpec((tm, D), lambda j, i: (i, 0)),
        pl.BlockSpec((None, D, D), lambda j, i: (layer, 0, j)),
    ]
    args = [h, w]
    aliases = {}
    if rows_so_far is not None:
        in_specs += [pl.BlockSpec(memory_space=pl.ANY)] * 2
        args += list(rows_so_far)
        aliases = {2: 1, 3: 2}
    vmem = (2 * _nbytes((tm, D), BF16) + 2 * _nbytes((D, D), BF16)
            + 2 * _nbytes((tm, D), BF16) + 4 * _nbytes((tm, D), F32) + 2 * _nbytes((tm, D), F32))
    return pl.pallas_call(
        functools.partial(_diff_in_body, q_scale=(dv // 2) ** -0.5 * LOG2E),
        out_shape=[jax.ShapeDtypeStruct((T, 3 * D), BF16), rows, rows],
        grid=(3, n_tiles),
        in_specs=in_specs,
        out_specs=[pl.BlockSpec((tm, D), lambda j, i: (i, j)), rows_spec(1), rows_spec(2)],
        input_output_aliases=aliases,
        compiler_params=_cparams(("arbitrary", "arbitrary"), vmem),
        name=name,
    )(*args)


def _proj_norm_res_body(m_ref, w_ref, x_ref, g_ref, o_ref):
    y = jnp.dot(m_ref[...], w_ref[...], preferred_element_type=F32)
    o_ref[...] = x_ref[...] + _rms(y, g_ref[...])


def _proj_norm_res(m, w, layer, x, g, *, tm_want=512, name):
    T, D = x.shape
    K = m.shape[1]
    tm = _pick_tile(T, tm_want)
    vmem = (2 * _nbytes((tm, K), BF16) + 2 * _nbytes((K, D), BF16) + 5 * _nbytes((tm, D), F32))
    return pl.pallas_call(
        _proj_norm_res_body,
        out_shape=jax.ShapeDtypeStruct((T, D), F32),
        grid=(T // tm,),
        in_specs=[
            pl.BlockSpec((tm, K), lambda i: (i, 0)),
            pl.BlockSpec((None, K, D), lambda i: (layer, 0, 0)),
            pl.BlockSpec((tm, D), lambda i: (i, 0)),
            pl.BlockSpec((1, D), lambda i: (0, 0)),
        ],
        out_specs=pl.BlockSpec((tm, D), lambda i: (i, 0)),
        compiler_params=_cparams(("parallel",), vmem),
        name=name,
    )(m, w, x, g.reshape(1, D))


def _ffn_body(x_ref, gpre_ref, wg_ref, wu_ref, wd_ref, gpost_ref, *rest):
    with_next = len(rest) == 5
    o_ref = rest[1] if with_next else rest[0]
    h_ref, acc_ref = rest[-2:]
    f = pl.program_id(1)

    @pl.when(f == 0)
    def _():
        h_ref[...] = _rms(x_ref[...], gpre_ref[...]).astype(BF16)
        acc_ref[...] = jnp.zeros_like(acc_ref)

    h = h_ref[...]
    gate = jnp.dot(h, wg_ref[...], preferred_element_type=F32)
    up = jnp.dot(h, wu_ref[...], preferred_element_type=F32)
    act = (gate * jax.nn.sigmoid(gate) * up).astype(BF16)
    acc_ref[...] += jnp.dot(act, wd_ref[...], preferred_element_type=F32)

    @pl.when(f == pl.num_programs(1) - 1)
    def _():
        out = x_ref[...] + _rms(acc_ref[...], gpost_ref[...])
        o_ref[...] = out
        if with_next:
            rest[2][...] = _rms(out, rest[0][...]).astype(BF16)


def _ffn(x, g_pre, w_gu, w_down, layer, g_post, g_next=None, *, tm_want=512, tf_want=512, name):
    T, D = x.shape
    F = w_down.shape[1]
    tm = _pick_tile(T, tm_want)
    tf = _pick_tile(F, tf_want)
    nf = F // tf
    row = pl.BlockSpec((tm, D), lambda i, f: (i, 0))
    vec = pl.BlockSpec((1, D), lambda i, f: (0, 0))
    in_specs = [row, vec,
                pl.BlockSpec((None, D, tf), lambda i, f: (layer, 0, f)),
                pl.BlockSpec((None, D, tf), lambda i, f: (layer, 0, f + nf)),
                pl.BlockSpec((None, tf, D), lambda i, f: (layer, f, 0)),
                vec]
    args = [x, g_pre.reshape(1, D), w_gu, w_gu, w_down, g_post.reshape(1, D)]
    out_shape = [jax.ShapeDtypeStruct((T, D), F32)]
    out_specs = [row]
    if g_next is not None:
        in_specs.append(vec)
        args.append(g_next.reshape(1, D))
        out_shape.append(jax.ShapeDtypeStruct((T, D), BF16))
        out_specs.append(row)
    vmem = (4 * _nbytes((tm, D), F32) + 3 * _nbytes((tm, D), BF16) + _nbytes((tm, D), F32)
            + 4 * _nbytes((D, tf), BF16) + 2 * _nbytes((tf, D), BF16) + 4 * _nbytes((tm, tf), F32))
    outs = pl.pallas_call(
        _ffn_body,
        out_shape=out_shape,
        grid=(T // tm, nf),
        in_specs=in_specs,
        out_specs=out_specs,
        scratch_shapes=[pltpu.VMEM((tm, D), BF16), pltpu.VMEM((tm, D), F32)],
        compiler_params=_cparams(("parallel", "arbitrary"), vmem),
        name=name,
    )(*args)
    return outs if g_next is not None else outs[0]


def _rotary(x, cos, sin):
    half = x.shape[-1] // 2
    x1, x2 = x[:, :half], x[:, half:]
    return jnp.concatenate([x1 * cos - x2 * sin, x1 * sin + x2 * cos], axis=-1)


def _retention_body(lg_ref, q_ref, k_ref, v_ref, g_ref, cos_ref, sin_ref, gn_ref, *rest,
                    has_state, k_scale):
    if has_state:
        s0_ref, o_ref, sout_ref, s_ref, dmat_ref, qdec_ref, kdec_ref = rest
    else:
        o_ref, sout_ref, s_ref, dmat_ref, qdec_ref, kdec_ref = rest
    hg = pl.program_id(1)
    c = pl.program_id(2)
    n_heads, dk, _ = s_ref.shape
    L = q_ref.shape[0]

    @pl.when(c == 0)
    def _():
        if has_state:
            s_ref[...] = s0_ref[0]
        else:
            s_ref[...] = jnp.zeros_like(s_ref)
        row = lax.broadcasted_iota(jnp.int32, (L, L), 0)
        col = lax.broadcasted_iota(jnp.int32, (L, L), 1)
        rel = (row - col).astype(F32)
        idx = lax.broadcasted_iota(jnp.int32, (L, dk), 0).astype(F32)
        for h in range(n_heads):
            lg = lg_ref[hg * n_heads + h]
            dmat_ref[h] = jnp.where(row >= col, jnp.exp(rel * lg), 0.0)
            qdec_ref[h] = jnp.exp((idx + 1.0) * lg)
            kdec_ref[h] = jnp.exp((L - 1.0 - idx) * lg)

    cos = cos_ref[...]
    sin = sin_ref[...]
    for h in range(n_heads):
        cols = slice(h * dk, (h + 1) * dk)
        lg = lg_ref[hg * n_heads + h]
        qr = _rotary(q_ref[:, cols].astype(F32), cos, sin)
        kr = _rotary(k_ref[:, cols].astype(F32), cos, sin) * k_scale
        v = v_ref[:, cols]
        qb = qr.astype(BF16)
        state = s_ref[h]

        scores = lax.dot_general(qb, kr.astype(BF16), (((1,), (1,)), ((), ())),
                                 preferred_element_type=F32) * dmat_ref[h]
        o = (jnp.dot(scores.astype(BF16), v, preferred_element_type=F32)
             + jnp.dot(qb, state.astype(BF16), preferred_element_type=F32) * qdec_ref[h])
        kd = (kr * kdec_ref[h]).astype(BF16)
        chunk_dec = jnp.exp(jnp.full((1, dk), L, F32) * lg)
        s_ref[h] = state * chunk_dec + lax.dot_general(kd, v, (((0,), (0,)), ((), ())),
                                                       preferred_element_type=F32)
        gate = g_ref[:, cols].astype(F32)
        o_ref[:, cols] = (gate * jax.nn.sigmoid(gate) * _rms(o, gn_ref[:, cols])).astype(o_ref.dtype)

    @pl.when(c == pl.num_programs(2) - 1)
    def _():
        sout_ref[0] = s_ref[...]


def _retention(qkvg, cos, sin, gn_g, lgs, state, *, batch, heads, heads_per_step=8, name):
    T, D4 = qkvg.shape
    D = D4 // 4
    dk = D // heads
    S = T // batch
    L = _pick_tile(S, RET_CHUNK)
    nc = S // L
    hps = _pick_tile(heads, heads_per_step)
    n_groups = heads // hps
    w = hps * dk
    has_state = state is not None

    def col(part):
        return lambda b, hg, c: (b * nc + c, part * n_groups + hg)

    state_spec = pl.BlockSpec((1, hps, dk, dk), lambda b, hg, c: (b, hg, 0, 0))
    in_specs = [
        pl.BlockSpec(memory_space=pltpu.SMEM),
        pl.BlockSpec((L, w), col(0)),
        pl.BlockSpec((L, w), col(1)),
        pl.BlockSpec((L, w), col(2)),
        pl.BlockSpec((L, w), col(3)),
        pl.BlockSpec((L, dk // 2), lambda b, hg, c: (c, 0)),
        pl.BlockSpec((L, dk // 2), lambda b, hg, c: (c, 0)),
        pl.BlockSpec((1, w), lambda b, hg, c: (0, hg)),
    ]
    args = [lgs, qkvg, qkvg, qkvg, qkvg, cos, sin, gn_g.reshape(1, D)]
    if has_state:
        in_specs.append(state_spec)
        args.append(state)
    vmem = (10 * _nbytes((L, w), BF16) + 4 * _nbytes((L, dk // 2), F32)
            + hps * (5 * _nbytes((dk, dk), F32) + _nbytes((L, L), F32) + 2 * _nbytes((L, dk), F32))
            + 16 * _nbytes((L, dk), F32))
    return pl.pallas_call(
        functools.partial(_retention_body, has_state=has_state, k_scale=dk ** -0.5),
        out_shape=[jax.ShapeDtypeStruct((T, D), BF16),
                   jax.ShapeDtypeStruct((batch, heads, dk, dk), F32)],
        grid=(batch, n_groups, nc),
        in_specs=in_specs,
        out_specs=[pl.BlockSpec((L, w), lambda b, hg, c: (b * nc + c, hg)), state_spec],
        scratch_shapes=[pltpu.VMEM((hps, dk, dk), F32), pltpu.VMEM((hps, L, L), F32),
                        pltpu.VMEM((hps, L, dk), F32), pltpu.VMEM((hps, L, dk), F32)],
        compiler_params=_cparams(("parallel", "parallel", "arbitrary"), vmem),
        name=name,
    )(*args)


def _t5_bucket(rel):
    nb = N_BUCKETS // 2
    max_exact = nb // 2
    bucket = jnp.where(rel > 0, nb, 0)
    n = jnp.abs(rel)
    large = max_exact + (jnp.log(jnp.maximum(n, 1).astype(F32) / max_exact)
                         / math.log(MAX_DISTANCE / max_exact) * (nb - max_exact)).astype(jnp.int32)
    large = jnp.minimum(large, nb - 1)
    return bucket + jnp.where(n < max_exact, n, large)


def _bias_vec_body(bucket_ref, rb_ref, o_ref):
    bkt = bucket_ref[...]
    n_buckets, heads = rb_ref.shape
    for h in range(heads):
        acc = jnp.zeros(bkt.shape, F32)
        for b in range(n_buckets):
            acc = acc + jnp.where(bkt == b, rb_ref[b, h], 0.0)
        o_ref[h:h + 1, :] = acc


def _bias_vec(rel_bias):
    heads = rel_bias.shape[1]
    rel = jnp.arange(BIAS_VEC_LEN, dtype=jnp.int32) - (BIAS_VEC_LEN - CHUNK)
    bucket = _t5_bucket(rel).reshape(1, BIAS_VEC_LEN)
    return pl.pallas_call(
        _bias_vec_body,
        out_shape=jax.ShapeDtypeStruct((heads, BIAS_VEC_LEN), F32),
        in_specs=[pl.BlockSpec(memory_space=pltpu.VMEM), pl.BlockSpec(memory_space=pltpu.SMEM)],
        out_specs=pl.BlockSpec(memory_space=pltpu.VMEM),
        name="t5_bias_lookup",
    )(bucket, rel_bias.astype(F32))


def _bias_tiles_body(vec_ref, o_ref, *, bases):
    _, _, tq, tk = o_ref.shape
    base, width = _bias_window(tq, tk)
    r0 = BIAS_VEC_LEN - CHUNK
    vec = vec_ref[0]
    first = vec[:, 0:1]
    last = vec[:, BIAS_VEC_LEN - 1:BIAS_VEC_LEN]
    padded = vec
    if width > BIAS_VEC_LEN:
        padded = jnp.concatenate([vec, jnp.broadcast_to(last, (1, width - BIAS_VEC_LEN))], axis=1)
    slot = lax.broadcasted_iota(jnp.int32, (1, width), 1)
    row = lax.broadcasted_iota(jnp.int32, (tq, tk), 0)
    col = lax.broadcasted_iota(jnp.int32, (tq, tk), 1)
    for t, (q0, k0) in enumerate(bases):
        start = r0 - base - (q0 - k0)
        idx = slot + start
        window = pltpu.roll(padded, (-start) % width, 1)
        window = jnp.where(idx < 0, first, jnp.where(idx >= BIAS_VEC_LEN, last, window))
        skew = pltpu.roll(jnp.broadcast_to(window, (tq, width)), 0, 1, stride=1, stride_axis=0)
        visible = (col + k0) // CHUNK <= (row + q0) // CHUNK
        o_ref[0, t] = jnp.where(visible, skew[:, base:base + tk] * LOG2E, MASKED)


def _bias_window(tq, tk):
    base = -(-tq // LANE) * LANE
    return base, max(-(-(base + tk) // LANE) * LANE, BIAS_VEC_LEN)


def _bias_tiles(vec, bases, tq, tk, *, name):
    heads = vec.shape[0]
    n = len(bases)
    _, width = _bias_window(tq, tk)
    vmem = 2 * _nbytes((n, tq, tk), F32) + 3 * _nbytes((tq, width), F32) + 4 * _nbytes((tq, tk), F32)
    return pl.pallas_call(
        functools.partial(_bias_tiles_body, bases=tuple(bases)),
        out_shape=jax.ShapeDtypeStruct((heads, n, tq, tk), F32),
        grid=(heads,),
        in_specs=[pl.BlockSpec((1, 1, BIAS_VEC_LEN), lambda h: (h, 0, 0))],
        out_specs=pl.BlockSpec((1, n, tq, tk), lambda h: (h, 0, 0, 0)),
        compiler_params=_cparams(("parallel",), vmem),
        name=name,
    )(vec.reshape(heads, 1, BIAS_VEC_LEN))


def _lambda(lam_ref, lambda_init):
    lv = lam_ref[...]
    a = jnp.sum(lv[0:1] * lv[1:2], axis=-1, keepdims=True)
    b = jnp.sum(lv[2:3] * lv[3:4], axis=-1, keepdims=True)
    return jnp.exp(a) - jnp.exp(b) + lambda_init


def _diff_finish(acc1, l1, acc2, l2, lam, g, lambda_init):
    o = acc1 * (1.0 / l1) - lam * (acc2 * (1.0 / l2))
    return _rms(o, g) * (1.0 - lambda_init)


def _diff_prompt_body(qb_ref, kb_ref, kind_ref, first_ref, last_ref,
                      q_ref, k_ref, v_ref, bias_ref, cfar_ref, lam_ref, g_ref, o_ref,
                      m_ref, l_ref, acc_ref, *, lambda_init, step_kinds, tk):
    h = pl.program_id(1)
    s = pl.program_id(2)
    dh = q_ref.shape[1] // 2
    kind = kind_ref[s]

    @pl.when(first_ref[s] == 1)
    def _():
        m_ref[...] = jnp.full_like(m_ref, MASKED)
        l_ref[...] = jnp.zeros_like(l_ref)
        acc_ref[...] = jnp.zeros_like(acc_ref)

    lane_tiles = [slice(c * LANE, (c + 1) * LANE) for c in range(tk // LANE)]

    def update(sub, tile, r0):
        keys = slice(sub * tk, (sub + 1) * tk)
        rows = slice(r0, q_ref.shape[0])
        v = v_ref[keys, :]
        shift = cfar_ref[h] * LOG2E if tile < 0 else 0.0
        for j in range(2):
            q = q_ref[rows, j * dh:(j + 1) * dh]
            k = k_ref[keys, j * dh:(j + 1) * dh]
            t = lax.dot_general(q, k, (((1,), (1,)), ((), ())), preferred_element_type=F32)
            if tile >= 0:
                t = t + bias_ref[0, tile, rows, :]
            part = t[:, lane_tiles[0]]
            for sl in lane_tiles[1:]:
                part = jnp.maximum(part, t[:, sl])
            m_old = m_ref[j, rows, :]
            m_new = jnp.maximum(m_old, jnp.max(part, axis=-1, keepdims=True) + shift)
            m_sub = m_new - shift
            alpha = jnp.exp2(m_old - m_new)
            ps = [jnp.exp2(t[:, sl] - m_sub) for sl in lane_tiles]
            l_ref[j, rows, :] = alpha * l_ref[j, rows, :] + functools.reduce(lambda a, b: a + b, ps)
            p = jnp.concatenate([x.astype(BF16) for x in ps], axis=1)
            alpha_v = jnp.concatenate([alpha] * (v.shape[1] // LANE), axis=1)
            acc_ref[j, rows, :] = (alpha_v * acc_ref[j, rows, :]
                                   + jnp.dot(p, v, preferred_element_type=F32))
            m_ref[j, rows, :] = m_new

    for t_id, kinds in enumerate(step_kinds):
        @pl.when(kind == t_id)
        def _(kinds=kinds):
            for sub, sub_kind in enumerate(kinds):
                if sub_kind is not None:
                    update(sub, *sub_kind)

    @pl.when(last_ref[s] == 1)
    def _():
        lam = _lambda(lam_ref, lambda_init)
        l1 = jnp.sum(l_ref[0], axis=-1, keepdims=True)
        l2 = jnp.sum(l_ref[1], axis=-1, keepdims=True)
        o_ref[...] = _diff_finish(acc_ref[0], l1, acc_ref[1], l2, lam, g_ref[...],
                                  lambda_init).astype(o_ref.dtype)


def _prompt_plan(S, tq_want=1024, tk_want=512, n_sub_want=4):
    tq = _pick_tile(S, tq_want)
    tk = _pick_tile(tq, tk_want)
    n_sub = _pick_tile(S // tk, n_sub_want)
    tks = tk * n_sub
    assert tq % CHUNK == 0 and tk % CHUNK == 0
    nq, nks = S // tq, S // tks

    def sub_kind(off):
        if off <= -tq:
            return None
        if off - (tk - 1) >= MAX_DISTANCE:
            return "far"
        return off

    pairs = [(a, b) for a in range(nq) for b in range(nks) if b * tks < (a + 1) * tq]
    pair_kinds = [tuple(sub_kind(a * tq - b * tks - sb * tk) for sb in range(n_sub))
                  for a, b in pairs]
    offs = sorted({o for kinds in pair_kinds for o in kinds if isinstance(o, int)})
    tile_of = {o: t for t, o in enumerate(offs)}

    def resolve(o):
        if o is None:
            return None
        if o == "far":
            return (-1, 0)
        return (tile_of[o], max(0, -o))

    step_kinds = sorted({tuple(resolve(o) for o in kinds) for kinds in pair_kinds},
                        key=lambda ks: repr(ks))
    kind_of = {ks: t for t, ks in enumerate(step_kinds)}
    qb = np.array([a for a, _ in pairs], np.int32)
    kb = np.array([b for _, b in pairs], np.int32)
    kind = np.array([kind_of[tuple(resolve(o) for o in kinds)] for kinds in pair_kinds], np.int32)
    first = np.array([int(b == 0) for _, b in pairs], np.int32)
    last = np.array([int(i + 1 == len(pairs) or pairs[i + 1][0] != a)
                     for i, (a, _) in enumerate(pairs)], np.int32)
    bases = tuple((max(o, 0), max(-o, 0)) for o in offs)
    return dict(tq=tq, tk=tk, tks=tks, nq=nq, nks=nks, bases=bases, step_kinds=tuple(step_kinds),
                tables=(qb, kb, kind, first, last))


def _diff_attention_prompt(qkv, tiles, cfar, lam_vecs, subln_g, plan, *, batch, heads, lambda_init,
                           name):
    T, D3 = qkv.shape
    D = D3 // 3
    dv = D // heads
    tq, tk, tks, nq, nks = (plan[k] for k in ("tq", "tk", "tks", "nq", "nks"))
    n_tiles = tiles.shape[1]
    n_steps = len(plan["tables"][0])

    def qmap(b, h, s, qb, kb, kind, first, last):
        return (b * nq + qb[s], h)

    def kmap(b, h, s, qb, kb, kind, first, last):
        return (b * nks + kb[s], heads + h)

    def vmap(b, h, s, qb, kb, kind, first, last):
        return (b * nks + kb[s], 2 * heads + h)

    vmem = (4 * _nbytes((tq, dv), BF16) + 8 * _nbytes((tks, dv), BF16)
            + 2 * _nbytes((n_tiles, tq, tk), F32) + 2 * _nbytes((tq, dv), F32)
            + 4 * _nbytes((tq, LANE), F32) + 10 * _nbytes((tq, tk), F32))
    grid_spec = pltpu.PrefetchScalarGridSpec(
        num_scalar_prefetch=5,
        grid=(batch, heads, n_steps),
        in_specs=[
            pl.BlockSpec((tq, dv), qmap),
            pl.BlockSpec((tks, dv), kmap),
            pl.BlockSpec((tks, dv), vmap),
            pl.BlockSpec((1, n_tiles, tq, tk), lambda b, h, s, *_: (h, 0, 0, 0)),
            pl.BlockSpec(memory_space=pltpu.SMEM),
            pl.BlockSpec(lam_vecs.shape, lambda b, h, s, *_: (0, 0)),
            pl.BlockSpec((1, dv), lambda b, h, s, *_: (0, 0)),
        ],
        out_specs=pl.BlockSpec((tq, dv), qmap),
        scratch_shapes=[pltpu.VMEM((2, tq, LANE), F32), pltpu.VMEM((2, tq, LANE), F32),
                        pltpu.VMEM((2, tq, dv), F32)],
    )
    return pl.pallas_call(
        functools.partial(_diff_prompt_body, lambda_init=lambda_init,
                          step_kinds=plan["step_kinds"], tk=tk),
        out_shape=jax.ShapeDtypeStruct((T, D), BF16),
        grid_spec=grid_spec,
        compiler_params=_cparams(("parallel", "parallel", "arbitrary"), vmem),
        name=name,
    )(*plan["tables"], qkv, qkv, qkv, tiles, cfar, lam_vecs, subln_g.reshape(1, dv))


def _diff_sample_body(q_ref, kn_ref, vn_ref, kc_ref, vc_ref, bc_ref, bn_ref, lam_ref, g_ref, o_ref,
                      m_ref, l_ref, acc_ref, *, lambda_init):
    c = pl.program_id(1)
    pc, heads, dv = kc_ref.shape
    dh = dv // 2
    nt = (((1,), (1,)), ((), ()))
    lane_tiles = [slice(i * LANE, (i + 1) * LANE) for i in range(pc // LANE)]

    @pl.when(c == 0)
    def _():
        m_ref[...] = jnp.full_like(m_ref, MASKED)
        l_ref[...] = jnp.zeros_like(l_ref)
        acc_ref[...] = jnp.zeros_like(acc_ref)

    kc = kc_ref[...].reshape(pc, heads * dv).astype(BF16)
    vc = vc_ref[...].reshape(pc, heads * dv).astype(BF16)
    for h in range(heads):
        v = vc[:, h * dv:(h + 1) * dv]
        for j in range(2):
            i = 2 * h + j
            cols = slice(h * dv + j * dh, h * dv + (j + 1) * dh)
            t = lax.dot_general(q_ref[:, cols], kc[:, cols], nt,
                                preferred_element_type=F32) + bc_ref[h]
            part = functools.reduce(jnp.maximum, [t[:, sl] for sl in lane_tiles])
            m_old = m_ref[i]
            m_new = jnp.maximum(m_old, jnp.max(part, axis=-1, keepdims=True))
            alpha = jnp.exp2(m_old - m_new)
            ps = [jnp.exp2(t[:, sl] - m_new) for sl in lane_tiles]
            l_ref[i] = alpha * l_ref[i] + functools.reduce(lambda a, b: a + b, ps)
            p = jnp.concatenate([x.astype(BF16) for x in ps], axis=1)
            alpha_v = jnp.concatenate([alpha] * (dv // LANE), axis=1)
            acc_ref[i] = alpha_v * acc_ref[i] + jnp.dot(p, v, preferred_element_type=F32)
            m_ref[i] = m_new

    @pl.when(c == pl.num_programs(1) - 1)
    def _():
        lam = _lambda(lam_ref, lambda_init)
        g = g_ref[...]
        for h in range(heads):
            vn = vn_ref[:, h * dv:(h + 1) * dv]
            accs, ls = [], []
            for j in range(2):
                i = 2 * h + j
                cols = slice(h * dv + j * dh, h * dv + (j + 1) * dh)
                t = lax.dot_general(q_ref[:, cols], kn_ref[:, cols], nt,
                                    preferred_element_type=F32) + bn_ref[h]
                m_old = m_ref[i][:, 0:1]
                m_new = jnp.maximum(m_old, jnp.max(t, axis=-1, keepdims=True))
                alpha = jnp.exp2(m_old - m_new)
                p = jnp.exp2(t - m_new)
                ls.append(alpha * jnp.sum(l_ref[i], axis=-1, keepdims=True)
                          + jnp.sum(p, axis=-1, keepdims=True))
                accs.append(alpha * acc_ref[i]
                            + jnp.dot(p.astype(BF16), vn, preferred_element_type=F32))
            o_ref[:, h * dv:(h + 1) * dv] = _diff_finish(
                accs[0], ls[0], accs[1], ls[1], lam, g, lambda_init).astype(o_ref.dtype)


def _diff_attention_sample(qkv, k_cache, v_cache, layer, bias_cache, bias_new, lam_vecs, subln_g, *,
                           lambda_init, pc_want=512, name):
    n_layers, batch, P, heads, dv = k_cache.shape
    D = heads * dv
    T = qkv.shape[0]
    n = T // batch
    pc = _pick_tile(P, pc_want)
    cache_spec = pl.BlockSpec((None, None, pc, heads, dv), lambda b, c: (layer, b, c, 0, 0))
    vmem = (6 * _nbytes((n, D), BF16) + 4 * _nbytes((pc, D), F32) + 4 * _nbytes((pc, D), F32)
            + 2 * _nbytes((heads, n, pc), F32) + 2 * _nbytes((n, D), BF16)
            + 2 * heads * (2 * _nbytes((n, LANE), F32) + _nbytes((n, dv), F32)))
    return pl.pallas_call(
        functools.partial(_diff_sample_body, lambda_init=lambda_init),
        out_shape=jax.ShapeDtypeStruct((T, D), BF16),
        grid=(batch, P // pc),
        in_specs=[
            pl.BlockSpec((n, D), lambda b, c: (b, 0)),
            pl.BlockSpec((n, D), lambda b, c: (b, 1)),
            pl.BlockSpec((n, D), lambda b, c: (b, 2)),
            cache_spec,
            cache_spec,
            pl.BlockSpec((heads, n, pc), lambda b, c: (0, 0, c)),
            pl.BlockSpec((heads, n, n), lambda b, c: (0, 0, 0)),
            pl.BlockSpec(lam_vecs.shape, lambda b, c: (0, 0)),
            pl.BlockSpec((1, dv), lambda b, c: (0, 0)),
        ],
        out_specs=pl.BlockSpec((n, D), lambda b, c: (b, 0)),
        scratch_shapes=[pltpu.VMEM((2 * heads, n, LANE), F32), pltpu.VMEM((2 * heads, n, LANE), F32),
                        pltpu.VMEM((2 * heads, n, dv), F32)],
        compiler_params=_cparams(("parallel", "arbitrary"), vmem),
        name=name,
    )(qkv, qkv, qkv, k_cache, v_cache, bias_cache, bias_new, lam_vecs, subln_g.reshape(1, dv))


def _rope_tables(pos, half):
    inv = ROPE_BASE ** (-jnp.arange(half, dtype=F32) / half)
    ang = pos.astype(F32)[:, None] * inv[None, :]
    return jnp.cos(ang), jnp.sin(ang)


def kernel(x_prompt, x_sample, state_ret, cache_k, cache_v, norm_g, ret_w_in, ret_w_out, ret_gn_g,
           diff_w_in, diff_w_out, diff_lambda, diff_subln_g, rel_bias, ffn_w_gu, ffn_w_down):
    B, S, D = x_prompt.shape
    DB, n_s, _ = x_sample.shape
    depth = norm_g.shape[0]
    ret_heads = state_ret.shape[2]
    diff_heads = cache_k.shape[3]
    past = cache_k.shape[2]

    pos_p = jnp.arange(S, dtype=jnp.int32)
    pos_s = past + jnp.arange(n_s, dtype=jnp.int32)
    half = D // ret_heads // 2
    rope_p = _rope_tables(pos_p, half)
    rope_s = _rope_tables(pos_s, half)
    lgs = jnp.log1p(-jnp.exp2(-5.0 - jnp.arange(ret_heads, dtype=F32)))
    bias_vec = _bias_vec(rel_bias)
    plan = _prompt_plan(S)
    tiles_p = _bias_tiles(bias_vec, plan["bases"], plan["tq"], plan["tk"], name="bias_tiles_prompt")
    bias_far = bias_vec[:, 0]
    bias_cache = _bias_tiles(bias_vec, [(past, 0)], n_s, past, name="bias_tiles_cache")[:, 0]
    bias_new = _bias_tiles(bias_vec, [(past, past)], n_s, n_s, name="bias_tiles_new")[:, 0]

    ret_w_in, ret_w_out, diff_w_in, diff_w_out, ffn_w_gu, ffn_w_down = (
        w.astype(BF16) for w in (ret_w_in, ret_w_out, diff_w_in, diff_w_out, ffn_w_gu, ffn_w_down))

    xp = x_prompt.reshape(B * S, D)
    xs = x_sample.reshape(DB * n_s, D)
    ret_p, ret_s = [], []
    rows_p = rows_s = None
    n_diff = depth // N_MIXERS
    for i in range(depth):
        g = norm_g[i]
        j = i // N_MIXERS
        if i % N_MIXERS == 0:
            w_out = ret_w_out
            qp = _norm_matmul(xp, g[0], ret_w_in, j, name=f"ret_in_p{i}")
            qs = _norm_matmul(xs, g[0], ret_w_in, j, name=f"ret_in_s{i}")
            mp, sp = _retention(qp, *rope_p, ret_gn_g[j], lgs, None, batch=B, heads=ret_heads,
                                name=f"retention_p{i}")
            ms, ss = _retention(qs, *rope_s, ret_gn_g[j], lgs, state_ret[j], batch=DB,
                                heads=ret_heads, name=f"retention_s{i}")
            ret_p.append(sp)
            ret_s.append(ss)
        else:
            lambda_init = 0.8 - 0.6 * math.exp(-0.3 * i)
            w_out = diff_w_out
            qkv_p, *rows_p = _diff_in(hp, diff_w_in, rows_p, heads=diff_heads, n_layers=n_diff,
                                      layer=j, name=f"diff_in_p{i}")
            qkv_s, *rows_s = _diff_in(hs, diff_w_in, rows_s, heads=diff_heads, n_layers=n_diff,
                                      layer=j, name=f"diff_in_s{i}")
            mp = _diff_attention_prompt(qkv_p, tiles_p, bias_far, diff_lambda[j], diff_subln_g[j],
                                        plan, batch=B, heads=diff_heads, lambda_init=lambda_init,
                                        name=f"diff_attn_p{i}")
            ms = _diff_attention_sample(qkv_s, cache_k, cache_v, j, bias_cache, bias_new,
                                        diff_lambda[j], diff_subln_g[j], lambda_init=lambda_init,
                                        name=f"diff_attn_s{i}")
        xp = _proj_norm_res(mp, w_out, j, xp, g[1], name=f"mix_out_p{i}")
        xs = _proj_norm_res(ms, w_out, j, xs, g[1], name=f"mix_out_s{i}")
        if i + 1 < depth and (i + 1) % N_MIXERS != 0:
            g_next = norm_g[i + 1, 0]
            xp, hp = _ffn(xp, g[2], ffn_w_gu, ffn_w_down, i, g[3], g_next, name=f"ffn_p{i}")
            xs, hs = _ffn(xs, g[2], ffn_w_gu, ffn_w_down, i, g[3], g_next, name=f"ffn_s{i}")
        else:
            xp = _ffn(xp, g[2], ffn_w_gu, ffn_w_down, i, g[3], name=f"ffn_p{i}")
            xs = _ffn(xs, g[2], ffn_w_gu, ffn_w_down, i, g[3], name=f"ffn_s{i}")

    dvh = D // diff_heads
    return (xp.reshape(B, S, D), xs.reshape(DB, n_s, D),
            jnp.stack(ret_p), jnp.stack(ret_s),
            rows_p[0].reshape(n_diff, B, S, diff_heads, dvh),
            rows_p[1].reshape(n_diff, B, S, diff_heads, dvh),
            rows_s[0].reshape(n_diff, DB, n_s, diff_heads, dvh),
            rows_s[1].reshape(n_diff, DB, n_s, diff_heads, dvh))
```

```python
import functools
import math

import numpy as np
import jax
import jax.numpy as jnp
from jax import lax
from jax.experimental import pallas as pl
from jax.experimental.pallas import tpu as pltpu

F32 = jnp.float32
BF16 = jnp.bfloat16

NORM_EPS = 1e-6
ROPE_BASE = 10000.0
CHUNK = 64
N_BUCKETS = 32
MAX_DISTANCE = 128
N_MIXERS = 2
LOG2E = math.log2(math.e)
MASKED = -1e30

V7X_VMEM_BYTES = 64 * 1024 * 1024
VMEM_HEADROOM = 6 * 1024 * 1024
LANE = 128

RET_CHUNK = 256
BIAS_VEC_LEN = 512


def _cparams(semantics, vmem_estimate):
    limit = min(V7X_VMEM_BYTES - VMEM_HEADROOM, max(32 * 1024 * 1024, int(vmem_estimate * 1.25)))
    return pltpu.CompilerParams(dimension_semantics=semantics, vmem_limit_bytes=limit)


def _nbytes(shape, dtype):
    return int(np.prod(shape)) * jnp.dtype(dtype).itemsize


def _pick_tile(n, want):
    t = min(n, want)
    while n % t:
        t //= 2
    return t


def _rms(x, g):
    ms = jnp.mean(x * x, axis=-1, keepdims=True)
    return x * lax.rsqrt(ms + NORM_EPS) * g


def _norm_matmul_body(x_ref, g_ref, w_ref, o_ref, h_ref):
    @pl.when(pl.program_id(1) == 0)
    def _():
        h_ref[...] = _rms(x_ref[...], g_ref[...]).astype(BF16)

    o_ref[...] = jnp.dot(h_ref[...], w_ref[...], preferred_element_type=F32).astype(o_ref.dtype)


def _norm_matmul(x, g, w, layer, *, tm_want=1024, tn_want=1024, name):
    T, D = x.shape
    N = w.shape[2]
    tm = _pick_tile(T, tm_want)
    tn = _pick_tile(N, tn_want)
    vmem = (2 * _nbytes((tm, D), F32) + _nbytes((tm, D), BF16) + 2 * _nbytes((D, tn), BF16)
            + 2 * _nbytes((tm, tn), BF16) + _nbytes((tm, tn), F32))
    return pl.pallas_call(
        _norm_matmul_body,
        out_shape=jax.ShapeDtypeStruct((T, N), BF16),
        grid=(T // tm, N // tn),
        in_specs=[
            pl.BlockSpec((tm, D), lambda i, j: (i, 0)),
            pl.BlockSpec((1, D), lambda i, j: (0, 0)),
            pl.BlockSpec((None, D, tn), lambda i, j: (layer, 0, j)),
        ],
        out_specs=pl.BlockSpec((tm, tn), lambda i, j: (i, j)),
        scratch_shapes=[pltpu.VMEM((tm, D), BF16)],
        compiler_params=_cparams(("parallel", "arbitrary"), vmem),
        name=name,
    )(x, g.reshape(1, D), w)


def _diff_in_body(h_ref, w_ref, *rest, q_scale):
    qkv_ref, kf_ref, vf_ref = rest[-3:]
    part = pl.program_id(0)

    def project():
        return jnp.dot(h_ref[...], w_ref[...], preferred_element_type=F32)

    @pl.when(part == 0)
    def _():
        qkv_ref[...] = (project() * q_scale).astype(BF16)

    @pl.when(part == 1)
    def _():
        y = project()
        qkv_ref[...] = y.astype(BF16)
        kf_ref[...] = y.reshape(kf_ref.shape)

    @pl.when(part == 2)
    def _():
        y = project()
        qkv_ref[...] = y.astype(BF16)
        vf_ref[...] = y.reshape(vf_ref.shape)


def _diff_in(h, w, rows_so_far, *, heads, n_layers, layer, tm_want=512, name):
    T, D = h.shape
    dv = D // heads
    tm = _pick_tile(T, tm_want)
    n_tiles = T // tm
    rows = jax.ShapeDtypeStruct((n_layers, T, heads, dv), F32)

    def rows_spec(part):
        def index(j, i):
            return (layer, jnp.where(j == part, i, jnp.where(j < part, 0, n_tiles - 1)), 0, 0)
        return pl.BlockSpec((None, tm, heads, dv), index)

    in_specs = [
        pl.BlockSpec((tm, D), lambda j, i: (i, 0)),
        pl.BlockSpec((None, D, D), lambda j, i: (layer, 0, j)),
    ]
    args = [h, w]
    aliases = {}
    if rows_so_far is not None:
        in_specs += [pl.BlockSpec(memory_space=pl.ANY)] * 2
        args += list(rows_so_far)
        aliases = {2: 1, 3: 2}
    vmem = (2 * _nbytes((tm, D), BF16) + 2 * _nbytes((D, D), BF16)
            + 2 * _nbytes((tm, D), BF16) + 4 * _nbytes((tm, D), F32) + 2 * _nbytes((tm, D), F32))
    return pl.pallas_call(
        functools.partial(_diff_in_body, q_scale=(dv // 2) ** -0.5 * LOG2E),
        out_shape=[jax.ShapeDtypeStruct((T, 3 * D), BF16), rows, rows],
        grid=(3, n_tiles),
        in_specs=in_specs,
        out_specs=[pl.BlockSpec((tm, D), lambda j, i: (i, j)), rows_spec(1), rows_spec(2)],
        input_output_aliases=aliases,
        compiler_params=_cparams(("arbitrary", "arbitrary"), vmem),
        name=name,
    )(*args)


def _proj_norm_res_body(m_ref, w_ref, x_ref, g_ref, gnext_ref, o_ref, h_ref):
    y = jnp.dot(m_ref[...], w_ref[...], preferred_element_type=F32)
    out = x_ref[...] + _rms(y, g_ref[...])
    o_ref[...] = out
    h_ref[...] = _rms(out, gnext_ref[...]).astype(BF16)


def _proj_norm_res(m, w, layer, x, g, g_next, *, tm_want=512, name):
    T, D = x.shape
    K = m.shape[1]
    tm = _pick_tile(T, tm_want)
    row = pl.BlockSpec((tm, D), lambda i: (i, 0))
    vec = pl.BlockSpec((1, D), lambda i: (0, 0))
    vmem = (2 * _nbytes((tm, K), BF16) + 2 * _nbytes((K, D), BF16) + 5 * _nbytes((tm, D), F32)
            + 2 * _nbytes((tm, D), BF16))
    return pl.pallas_call(
        _proj_norm_res_body,
        out_shape=[jax.ShapeDtypeStruct((T, D), F32), jax.ShapeDtypeStruct((T, D), BF16)],
        grid=(T // tm,),
        in_specs=[
            pl.BlockSpec((tm, K), lambda i: (i, 0)),
            pl.BlockSpec((None, K, D), lambda i: (layer, 0, 0)),
            row, vec, vec,
        ],
        out_specs=[row, row],
        compiler_params=_cparams(("parallel",), vmem),
        name=name,
    )(m, w, x, g.reshape(1, D), g_next.reshape(1, D))


def _ffn_body(x_ref, h_ref, wg_ref, wu_ref, wd_ref, gpost_ref, *rest):
    with_next = len(rest) == 4
    o_ref = rest[1] if with_next else rest[0]
    acc_ref = rest[-1]
    f = pl.program_id(1)

    @pl.when(f == 0)
    def _():
        acc_ref[...] = jnp.zeros_like(acc_ref)

    h = h_ref[...]
    gate = jnp.dot(h, wg_ref[...], preferred_element_type=F32)
    up = jnp.dot(h, wu_ref[...], preferred_element_type=F32)
    act = (gate * jax.nn.sigmoid(gate) * up).astype(BF16)
    acc_ref[...] += jnp.dot(act, wd_ref[...], preferred_element_type=F32)

    @pl.when(f == pl.num_programs(1) - 1)
    def _():
        out = x_ref[...] + _rms(acc_ref[...], gpost_ref[...])
        o_ref[...] = out
        if with_next:
            rest[2][...] = _rms(out, rest[0][...]).astype(BF16)


def _ffn(x, h, w_gu, w_down, layer, g_post, g_next=None, *, tm_want=512, tf_want=512, name):
    T, D = x.shape
    F = w_down.shape[1]
    tm = _pick_tile(T, tm_want)
    tf = _pick_tile(F, tf_want)
    nf = F // tf
    row = pl.BlockSpec((tm, D), lambda i, f: (i, 0))
    vec = pl.BlockSpec((1, D), lambda i, f: (0, 0))
    in_specs = [row, row,
                pl.BlockSpec((None, D, tf), lambda i, f: (layer, 0, f)),
                pl.BlockSpec((None, D, tf), lambda i, f: (layer, 0, f + nf)),
                pl.BlockSpec((None, tf, D), lambda i, f: (layer, f, 0)),
                vec]
    args = [x, h, w_gu, w_gu, w_down, g_post.reshape(1, D)]
    out_shape = [jax.ShapeDtypeStruct((T, D), F32)]
    out_specs = [row]
    if g_next is not None:
        in_specs.append(vec)
        args.append(g_next.reshape(1, D))
        out_shape.append(jax.ShapeDtypeStruct((T, D), BF16))
        out_specs.append(row)
    vmem = (4 * _nbytes((tm, D), F32) + 3 * _nbytes((tm, D), BF16) + _nbytes((tm, D), F32)
            + 4 * _nbytes((D, tf), BF16) + 2 * _nbytes((tf, D), BF16) + 4 * _nbytes((tm, tf), F32))
    outs = pl.pallas_call(
        _ffn_body,
        out_shape=out_shape,
        grid=(T // tm, nf),
        in_specs=in_specs,
        out_specs=out_specs,
        scratch_shapes=[pltpu.VMEM((tm, D), F32)],
        compiler_params=_cparams(("parallel", "arbitrary"), vmem),
        name=name,
    )(*args)
    return outs if g_next is not None else outs[0]


def _rotary(x, cos, sin):
    half = x.shape[-1] // 2
    x1, x2 = x[:, :half], x[:, half:]
    return jnp.concatenate([x1 * cos - x2 * sin, x1 * sin + x2 * cos], axis=-1)


def _retention_body(lg_ref, q_ref, k_ref, v_ref, g_ref, cos_ref, sin_ref, gn_ref, *rest,
                    has_state, k_scale):
    if has_state:
        s0_ref, o_ref, sout_ref, s_ref, dmat_ref, qdec_ref, kdec_ref = rest
    else:
        o_ref, sout_ref, s_ref, dmat_ref, qdec_ref, kdec_ref = rest
    hg = pl.program_id(1)
    c = pl.program_id(2)
    n_heads, dk, _ = s_ref.shape
    L = q_ref.shape[0]

    @pl.when(c == 0)
    def _():
        if has_state:
            s_ref[...] = s0_ref[0]
        else:
            s_ref[...] = jnp.zeros_like(s_ref)
        row = lax.broadcasted_iota(jnp.int32, (L, L), 0)
        col = lax.broadcasted_iota(jnp.int32, (L, L), 1)
        rel = (row - col).astype(F32)
        idx = lax.broadcasted_iota(jnp.int32, (L, dk), 0).astype(F32)
        for h in range(n_heads):
            lg = lg_ref[hg * n_heads + h]
            dmat_ref[h] = jnp.where(row >= col, jnp.exp(rel * lg), 0.0)
            qdec_ref[h] = jnp.exp((idx + 1.0) * lg)
            kdec_ref[h] = jnp.exp((L - 1.0 - idx) * lg)

    cos = cos_ref[...]
    sin = sin_ref[...]
    for h in range(n_heads):
        cols = slice(h * dk, (h + 1) * dk)
        lg = lg_ref[hg * n_heads + h]
        qr = _rotary(q_ref[:, cols].astype(F32), cos, sin)
        kr = _rotary(k_ref[:, cols].astype(F32), cos, sin) * k_scale
        v = v_ref[:, cols]
        qb = qr.astype(BF16)
        state = s_ref[h]

        scores = lax.dot_general(qb, kr.astype(BF16), (((1,), (1,)), ((), ())),
                                 preferred_element_type=F32) * dmat_ref[h]
        o = (jnp.dot(scores.astype(BF16), v, preferred_element_type=F32)
             + jnp.dot(qb, state.astype(BF16), preferred_element_type=F32) * qdec_ref[h])
        kd = (kr * kdec_ref[h]).astype(BF16)
        chunk_dec = jnp.exp(jnp.full((1, dk), L, F32) * lg)
        s_ref[h] = state * chunk_dec + lax.dot_general(kd, v, (((0,), (0,)), ((), ())),
                                                       preferred_element_type=F32)
        gate = g_ref[:, cols].astype(F32)
        o_ref[:, cols] = (gate * jax.nn.sigmoid(gate) * _rms(o, gn_ref[:, cols])).astype(o_ref.dtype)

    @pl.when(c == pl.num_programs(2) - 1)
    def _():
        sout_ref[0] = s_ref[...]


def _retention(qkvg, cos, sin, gn_g, lgs, state, *, batch, heads, heads_per_step=8, name):
    T, D4 = qkvg.shape
    D = D4 // 4
    dk = D // heads
    S = T // batch
    L = _pick_tile(S, RET_CHUNK)
    nc = S // L
    hps = _pick_tile(heads, heads_per_step)
    n_groups = heads // hps
    w = hps * dk
    has_state = state is not None

    def col(part):
        return lambda b, hg, c: (b * nc + c, part * n_groups + hg)

    state_spec = pl.BlockSpec((1, hps, dk, dk), lambda b, hg, c: (b, hg, 0, 0))
    in_specs = [
        pl.BlockSpec(memory_space=pltpu.SMEM),
        pl.BlockSpec((L, w), col(0)),
        pl.BlockSpec((L, w), col(1)),
        pl.BlockSpec((L, w), col(2)),
        pl.BlockSpec((L, w), col(3)),
        pl.BlockSpec((L, dk // 2), lambda b, hg, c: (c, 0)),
        pl.BlockSpec((L, dk // 2), lambda b, hg, c: (c, 0)),
        pl.BlockSpec((1, w), lambda b, hg, c: (0, hg)),
    ]
    args = [lgs, qkvg, qkvg, qkvg, qkvg, cos, sin, gn_g.reshape(1, D)]
    if has_state:
        in_specs.append(state_spec)
        args.append(state)
    vmem = (10 * _nbytes((L, w), BF16) + 4 * _nbytes((L, dk // 2), F32)
            + hps * (5 * _nbytes((dk, dk), F32) + _nbytes((L, L), F32) + 2 * _nbytes((L, dk), F32))
            + 16 * _nbytes((L, dk), F32))
    return pl.pallas_call(
        functools.partial(_retention_body, has_state=has_state, k_scale=dk ** -0.5),
        out_shape=[jax.ShapeDtypeStruct((T, D), BF16),
                   jax.ShapeDtypeStruct((batch, heads, dk, dk), F32)],
        grid=(batch, n_groups, nc),
        in_specs=in_specs,
        out_specs=[pl.BlockSpec((L, w), lambda b, hg, c: (b * nc + c, hg)), state_spec],
        scratch_shapes=[pltpu.VMEM((hps, dk, dk), F32), pltpu.VMEM((hps, L, L), F32),
                        pltpu.VMEM((hps, L, dk), F32), pltpu.VMEM((hps, L, dk), F32)],
        compiler_params=_cparams(("parallel", "parallel", "arbitrary"), vmem),
        name=name,
    )(*args)


def _t5_bucket(rel):
    nb = N_BUCKETS // 2
    max_exact = nb // 2
    bucket = jnp.where(rel > 0, nb, 0)
    n = jnp.abs(rel)
    large = max_exact + (jnp.log(jnp.maximum(n, 1).astype(F32) / max_exact)
                         / math.log(MAX_DISTANCE / max_exact) * (nb - max_exact)).astype(jnp.int32)
    large = jnp.minimum(large, nb - 1)
    return bucket + jnp.where(n < max_exact, n, large)


def _bias_vec_body(bucket_ref, rb_ref, o_ref):
    bkt = bucket_ref[...]
    n_buckets, heads = rb_ref.shape
    for h in range(heads):
        acc = jnp.zeros(bkt.shape, F32)
        for b in range(n_buckets):
            acc = acc + jnp.where(bkt == b, rb_ref[b, h], 0.0)
        o_ref[h:h + 1, :] = acc


def _bias_vec(rel_bias):
    heads = rel_bias.shape[1]
    rel = jnp.arange(BIAS_VEC_LEN, dtype=jnp.int32) - (BIAS_VEC_LEN - CHUNK)
    bucket = _t5_bucket(rel).reshape(1, BIAS_VEC_LEN)
    return pl.pallas_call(
        _bias_vec_body,
        out_shape=jax.ShapeDtypeStruct((heads, BIAS_VEC_LEN), F32),
        in_specs=[pl.BlockSpec(memory_space=pltpu.VMEM), pl.BlockSpec(memory_space=pltpu.SMEM)],
        out_specs=pl.BlockSpec(memory_space=pltpu.VMEM),
        name="t5_bias_lookup",
    )(bucket, rel_bias.astype(F32))


def _bias_tiles_body(vec_ref, o_ref, *, bases):
    _, _, tq, tk = o_ref.shape
    base, width = _bias_window(tq, tk)
    r0 = BIAS_VEC_LEN - CHUNK
    vec = vec_ref[0]
    first = vec[:, 0:1]
    last = vec[:, BIAS_VEC_LEN - 1:BIAS_VEC_LEN]
    padded = vec
    if width > BIAS_VEC_LEN:
        padded = jnp.concatenate([vec, jnp.broadcast_to(last, (1, width - BIAS_VEC_LEN))], axis=1)
    slot = lax.broadcasted_iota(jnp.int32, (1, width), 1)
    row = lax.broadcasted_iota(jnp.int32, (tq, tk), 0)
    col = lax.broadcasted_iota(jnp.int32, (tq, tk), 1)
    for t, (q0, k0) in enumerate(bases):
        start = r0 - base - (q0 - k0)
        idx = slot + start
        window = pltpu.roll(padded, (-start) % width, 1)
        window = jnp.where(idx < 0, first, jnp.where(idx >= BIAS_VEC_LEN, last, window))
        skew = pltpu.roll(jnp.broadcast_to(window, (tq, width)), 0, 1, stride=1, stride_axis=0)
        visible = (col + k0) // CHUNK <= (row + q0) // CHUNK
        o_ref[0, t] = jnp.where(visible, skew[:, base:base + tk] * LOG2E, MASKED)


def _bias_window(tq, tk):
    base = -(-tq // LANE) * LANE
    return base, max(-(-(base + tk) // LANE) * LANE, BIAS_VEC_LEN)


def _bias_tiles(vec, bases, tq, tk, *, name):
    heads = vec.shape[0]
    n = len(bases)
    _, width = _bias_window(tq, tk)
    vmem = 2 * _nbytes((n, tq, tk), F32) + 3 * _nbytes((tq, width), F32) + 4 * _nbytes((tq, tk), F32)
    return pl.pallas_call(
        functools.partial(_bias_tiles_body, bases=tuple(bases)),
        out_shape=jax.ShapeDtypeStruct((heads, n, tq, tk), F32),
        grid=(heads,),
        in_specs=[pl.BlockSpec((1, 1, BIAS_VEC_LEN), lambda h: (h, 0, 0))],
        out_specs=pl.BlockSpec((1, n, tq, tk), lambda h: (h, 0, 0, 0)),
        compiler_params=_cparams(("parallel",), vmem),
        name=name,
    )(vec.reshape(heads, 1, BIAS_VEC_LEN))


def _lambda(lam_ref, lambda_init):
    lv = lam_ref[...]
    a = jnp.sum(lv[0:1] * lv[1:2], axis=-1, keepdims=True)
    b = jnp.sum(lv[2:3] * lv[3:4], axis=-1, keepdims=True)
    return jnp.exp(a) - jnp.exp(b) + lambda_init


def _diff_finish(acc1, l1, acc2, l2, lam, g, lambda_init):
    o = acc1 * (1.0 / l1) - lam * (acc2 * (1.0 / l2))
    return _rms(o, g) * (1.0 - lambda_init)


def _diff_prompt_body(qb_ref, kb_ref, kind_ref, first_ref, last_ref,
                      q_ref, k_ref, v_ref, bias_ref, cfar_ref, lam_ref, g_ref, o_ref,
                      m_ref, l_ref, acc_ref, *, lambda_init, step_kinds, tk):
    h = pl.program_id(1)
    s = pl.program_id(2)
    dh = q_ref.shape[1] // 2
    kind = kind_ref[s]

    @pl.when(first_ref[s] == 1)
    def _():
        m_ref[...] = jnp.full_like(m_ref, MASKED)
        l_ref[...] = jnp.zeros_like(l_ref)
        acc_ref[...] = jnp.zeros_like(acc_ref)

    lane_tiles = [slice(c * LANE, (c + 1) * LANE) for c in range(tk // LANE)]

    def update(sub, tile, r0):
        keys = slice(sub * tk, (sub + 1) * tk)
        rows = slice(r0, q_ref.shape[0])
        v = v_ref[keys, :]
        shift = cfar_ref[h] * LOG2E if tile < 0 else 0.0
        for j in range(2):
            q = q_ref[rows, j * dh:(j + 1) * dh]
            k = k_ref[keys, j * dh:(j + 1) * dh]
            t = lax.dot_general(q, k, (((1,), (1,)), ((), ())), preferred_element_type=F32)
            if tile >= 0:
                t = t + bias_ref[0, tile, rows, :]
            part = t[:, lane_tiles[0]]
            for sl in lane_tiles[1:]:
                part = jnp.maximum(part, t[:, sl])
            m_old = m_ref[j, rows, :]
            m_new = jnp.maximum(m_old, jnp.max(part, axis=-1, keepdims=True) + shift)
            m_sub = m_new - shift
            alpha = jnp.exp2(m_old - m_new)
            ps = [jnp.exp2(t[:, sl] - m_sub) for sl in lane_tiles]
            l_ref[j, rows, :] = alpha * l_ref[j, rows, :] + functools.reduce(lambda a, b: a + b, ps)
            p = jnp.concatenate([x.astype(BF16) for x in ps], axis=1)
            alpha_v = jnp.concatenate([alpha] * (v.shape[1] // LANE), axis=1)
            acc_ref[j, rows, :] = (alpha_v * acc_ref[j, rows, :]
                                   + jnp.dot(p, v, preferred_element_type=F32))
            m_ref[j, rows, :] = m_new

    for t_id, kinds in enumerate(step_kinds):
        @pl.when(kind == t_id)
        def _(kinds=kinds):
            for sub, sub_kind in enumerate(kinds):
                if sub_kind is not None:
                    update(sub, *sub_kind)

    @pl.when(last_ref[s] == 1)
    def _():
        lam = _lambda(lam_ref, lambda_init)
        l1 = jnp.sum(l_ref[0], axis=-1, keepdims=True)
        l2 = jnp.sum(l_ref[1], axis=-1, keepdims=True)
        o_ref[...] = _diff_finish(acc_ref[0], l1, acc_ref[1], l2, lam, g_ref[...],
                                  lambda_init).astype(o_ref.dtype)


def _prompt_plan(S, tq_want=1024, tk_want=512, n_sub_want=4):
    tq = _pick_tile(S, tq_want)
    tk = _pick_tile(tq, tk_want)
    n_sub = _pick_tile(S // tk, n_sub_want)
    tks = tk * n_sub
    assert tq % CHUNK == 0 and tk % CHUNK == 0
    nq, nks = S // tq, S // tks

    def sub_kind(off):
        if off <= -tq:
            return None
        if off - (tk - 1) >= MAX_DISTANCE:
            return "far"
        return off

    pairs = [(a, b) for a in range(nq) for b in range(nks) if b * tks < (a + 1) * tq]
    pair_kinds = [tuple(sub_kind(a * tq - b * tks - sb * tk) for sb in range(n_sub))
                  for a, b in pairs]
    offs = sorted({o for kinds in pair_kinds for o in kinds if isinstance(o, int)})
    tile_of = {o: t for t, o in enumerate(offs)}

    def resolve(o):
        if o is None:
            return None
        if o == "far":
            return (-1, 0)
        return (tile_of[o], max(0, -o))

    step_kinds = sorted({tuple(resolve(o) for o in kinds) for kinds in pair_kinds},
                        key=lambda ks: repr(ks))
    kind_of = {ks: t for t, ks in enumerate(step_kinds)}
    qb = np.array([a for a, _ in pairs], np.int32)
    kb = np.array([b for _, b in pairs], np.int32)
    kind = np.array([kind_of[tuple(resolve(o) for o in kinds)] for kinds in pair_kinds], np.int32)
    first = np.array([int(b == 0) for _, b in pairs], np.int32)
    last = np.array([int(i + 1 == len(pairs) or pairs[i + 1][0] != a)
                     for i, (a, _) in enumerate(pairs)], np.int32)
    bases = tuple((max(o, 0), max(-o, 0)) for o in offs)
    return dict(tq=tq, tk=tk, tks=tks, nq=nq, nks=nks, bases=bases, step_kinds=tuple(step_kinds),
                tables=(qb, kb, kind, first, last))


def _diff_attention_prompt(qkv, tiles, cfar, lam_vecs, subln_g, plan, *, batch, heads, lambda_init,
                           name):
    T, D3 = qkv.shape
    D = D3 // 3
    dv = D // heads
    tq, tk, tks, nq, nks = (plan[k] for k in ("tq", "tk", "tks", "nq", "nks"))
    n_tiles = tiles.shape[1]
    n_steps = len(plan["tables"][0])

    def qmap(b, h, s, qb, kb, kind, first, last):
        return (b * nq + qb[s], h)

    def kmap(b, h, s, qb, kb, kind, first, last):
        return (b * nks + kb[s], heads + h)

    def vmap(b, h, s, qb, kb, kind, first, last):
        return (b * nks + kb[s], 2 * heads + h)

    vmem = (4 * _nbytes((tq, dv), BF16) + 8 * _nbytes((tks, dv), BF16)
            + 2 * _nbytes((n_tiles, tq, tk), F32) + 2 * _nbytes((tq, dv), F32)
            + 4 * _nbytes((tq, LANE), F32) + 10 * _nbytes((tq, tk), F32))
    grid_spec = pltpu.PrefetchScalarGridSpec(
        num_scalar_prefetch=5,
        grid=(batch, heads, n_steps),
        in_specs=[
            pl.BlockSpec((tq, dv), qmap),
            pl.BlockSpec((tks, dv), kmap),
            pl.BlockSpec((tks, dv), vmap),
            pl.BlockSpec((1, n_tiles, tq, tk), lambda b, h, s, *_: (h, 0, 0, 0)),
            pl.BlockSpec(memory_space=pltpu.SMEM),
            pl.BlockSpec(lam_vecs.shape, lambda b, h, s, *_: (0, 0)),
            pl.BlockSpec((1, dv), lambda b, h, s, *_: (0, 0)),
        ],
        out_specs=pl.BlockSpec((tq, dv), qmap),
        scratch_shapes=[pltpu.VMEM((2, tq, LANE), F32), pltpu.VMEM((2, tq, LANE), F32),
                        pltpu.VMEM((2, tq, dv), F32)],
    )
    return pl.pallas_call(
        functools.partial(_diff_prompt_body, lambda_init=lambda_init,
                          step_kinds=plan["step_kinds"], tk=tk),
        out_shape=jax.ShapeDtypeStruct((T, D), BF16),
        grid_spec=grid_spec,
        compiler_params=_cparams(("parallel", "parallel", "arbitrary"), vmem),
        name=name,
    )(*plan["tables"], qkv, qkv, qkv, tiles, cfar, lam_vecs, subln_g.reshape(1, dv))


def _diff_sample_body(q_ref, kn_ref, vn_ref, kc_ref, vc_ref, bc_ref, bn_ref, lam_ref, g_ref, o_ref,
                      m_ref, l_ref, acc_ref, *, lambda_init):
    c = pl.program_id(1)
    pc, heads, dv = kc_ref.shape
    dh = dv // 2
    nt = (((1,), (1,)), ((), ()))
    lane_tiles = [slice(i * LANE, (i + 1) * LANE) for i in range(pc // LANE)]

    @pl.when(c == 0)
    def _():
        m_ref[...] = jnp.full_like(m_ref, MASKED)
        l_ref[...] = jnp.zeros_like(l_ref)
        acc_ref[...] = jnp.zeros_like(acc_ref)

    kc = kc_ref[...].reshape(pc, heads * dv).astype(BF16)
    vc = vc_ref[...].reshape(pc, heads * dv).astype(BF16)
    for h in range(heads):
        v = vc[:, h * dv:(h + 1) * dv]
        for j in range(2):
            i = 2 * h + j
            cols = slice(h * dv + j * dh, h * dv + (j + 1) * dh)
            t = lax.dot_general(q_ref[:, cols], kc[:, cols], nt,
                                preferred_element_type=F32) + bc_ref[h]
            part = functools.reduce(jnp.maximum, [t[:, sl] for sl in lane_tiles])
            m_old = m_ref[i]
            m_new = jnp.maximum(m_old, jnp.max(part, axis=-1, keepdims=True))
            alpha = jnp.exp2(m_old - m_new)
            ps = [jnp.exp2(t[:, sl] - m_new) for sl in lane_tiles]
            l_ref[i] = alpha * l_ref[i] + functools.reduce(lambda a, b: a + b, ps)
            p = jnp.concatenate([x.astype(BF16) for x in ps], axis=1)
            alpha_v = jnp.concatenate([alpha] * (dv // LANE), axis=1)
            acc_ref[i] = alpha_v * acc_ref[i] + jnp.dot(p, v, preferred_element_type=F32)
            m_ref[i] = m_new

    @pl.when(c == pl.num_programs(1) - 1)
    def _():
        lam = _lambda(lam_ref, lambda_init)
        g = g_ref[...]
        for h in range(heads):
            vn = vn_ref[:, h * dv:(h + 1) * dv]
            accs, ls = [], []
            for j in range(2):
                i = 2 * h + j
                cols = slice(h * dv + j * dh, h * dv + (j + 1) * dh)
                t = lax.dot_general(q_ref[:, cols], kn_ref[:, cols], nt,
                                    preferred_element_type=F32) + bn_ref[h]
                m_old = m_ref[i][:, 0:1]
                m_new = jnp.maximum(m_old, jnp.max(t, axis=-1, keepdims=True))
                alpha = jnp.exp2(m_old - m_new)
                p = jnp.exp2(t - m_new)
                ls.append(alpha * jnp.sum(l_ref[i], axis=-1, keepdims=True)
                          + jnp.sum(p, axis=-1, keepdims=True))
                accs.append(alpha * acc_ref[i]
                            + jnp.dot(p.astype(BF16), vn, preferred_element_type=F32))
            o_ref[:, h * dv:(h + 1) * dv] = _diff_finish(
                accs[0], ls[0], accs[1], ls[1], lam, g, lambda_init).astype(o_ref.dtype)


def _diff_attention_sample(qkv, k_cache, v_cache, layer, bias_cache, bias_new, lam_vecs, subln_g, *,
                           lambda_init, pc_want=512, name):
    n_layers, batch, P, heads, dv = k_cache.shape
    D = heads * dv
    T = qkv.shape[0]
    n = T // batch
    pc = _pick_tile(P, pc_want)
    cache_spec = pl.BlockSpec((None, None, pc, heads, dv), lambda b, c: (layer, b, c, 0, 0))
    vmem = (6 * _nbytes((n, D), BF16) + 4 * _nbytes((pc, D), F32) + 4 * _nbytes((pc, D), F32)
            + 2 * _nbytes((heads, n, pc), F32) + 2 * _nbytes((n, D), BF16)
            + 2 * heads * (2 * _nbytes((n, LANE), F32) + _nbytes((n, dv), F32)))
    return pl.pallas_call(
        functools.partial(_diff_sample_body, lambda_init=lambda_init),
        out_shape=jax.ShapeDtypeStruct((T, D), BF16),
        grid=(batch, P // pc),
        in_specs=[
            pl.BlockSpec((n, D), lambda b, c: (b, 0)),
            pl.BlockSpec((n, D), lambda b, c: (b, 1)),
            pl.BlockSpec((n, D), lambda b, c: (b, 2)),
            cache_spec,
            cache_spec,
            pl.BlockSpec((heads, n, pc), lambda b, c: (0, 0, c)),
            pl.BlockSpec((heads, n, n), lambda b, c: (0, 0, 0)),
            pl.BlockSpec(lam_vecs.shape, lambda b, c: (0, 0)),
            pl.BlockSpec((1, dv), lambda b, c: (0, 0)),
        ],
        out_specs=pl.BlockSpec((n, D), lambda b, c: (b, 0)),
        scratch_shapes=[pltpu.VMEM((2 * heads, n, LANE), F32), pltpu.VMEM((2 * heads, n, LANE), F32),
                        pltpu.VMEM((2 * heads, n, dv), F32)],
        compiler_params=_cparams(("parallel", "arbitrary"), vmem),
        name=name,
    )(qkv, qkv, qkv, k_cache, v_cache, bias_cache, bias_new, lam_vecs, subln_g.reshape(1, dv))


def _rope_tables(pos, half):
    inv = ROPE_BASE ** (-jnp.arange(half, dtype=F32) / half)
    ang = pos.astype(F32)[:, None] * inv[None, :]
    return jnp.cos(ang), jnp.sin(ang)


def kernel(x_prompt, x_sample, state_ret, cache_k, cache_v, norm_g, ret_w_in, ret_w_out, ret_gn_g,
           diff_w_in, diff_w_out, diff_lambda, diff_subln_g, rel_bias, ffn_w_gu, ffn_w_down):
    B, S, D = x_prompt.shape
    DB, n_s, _ = x_sample.shape
    depth = norm_g.shape[0]
    ret_heads = state_ret.shape[2]
    diff_heads = cache_k.shape[3]
    past = cache_k.shape[2]

    pos_p = jnp.arange(S, dtype=jnp.int32)
    pos_s = past + jnp.arange(n_s, dtype=jnp.int32)
    half = D // ret_heads // 2
    rope_p = _rope_tables(pos_p, half)
    rope_s = _rope_tables(pos_s, half)
    lgs = jnp.log1p(-jnp.exp2(-5.0 - jnp.arange(ret_heads, dtype=F32)))
    bias_vec = _bias_vec(rel_bias)
    plan = _prompt_plan(S)
    tiles_p = _bias_tiles(bias_vec, plan["bases"], plan["tq"], plan["tk"], name="bias_tiles_prompt")
    bias_far = bias_vec[:, 0]
    bias_cache = _bias_tiles(bias_vec, [(past, 0)], n_s, past, name="bias_tiles_cache")[:, 0]
    bias_new = _bias_tiles(bias_vec, [(past, past)], n_s, n_s, name="bias_tiles_new")[:, 0]

    ret_w_in, ret_w_out, diff_w_in, diff_w_out, ffn_w_gu, ffn_w_down = (
        w.astype(BF16) for w in (ret_w_in, ret_w_out, diff_w_in, diff_w_out, ffn_w_gu, ffn_w_down))

    xp = x_prompt.reshape(B * S, D)
    xs = x_sample.reshape(DB * n_s, D)
    ret_p, ret_s = [], []
    rows_p = rows_s = None
    n_diff = depth // N_MIXERS
    for i in range(depth):
        g = norm_g[i]
        j = i // N_MIXERS
        if i % N_MIXERS == 0:
            w_out = ret_w_out
            qp = _norm_matmul(xp, g[0], ret_w_in, j, name=f"ret_in_p{i}")
            qs = _norm_matmul(xs, g[0], ret_w_in, j, name=f"ret_in_s{i}")
            mp, sp = _retention(qp, *rope_p, ret_gn_g[j], lgs, None, batch=B, heads=ret_heads,
                                name=f"retention_p{i}")
            ms, ss = _retention(qs, *rope_s, ret_gn_g[j], lgs, state_ret[j], batch=DB,
                                heads=ret_heads, name=f"retention_s{i}")
            ret_p.append(sp)
            ret_s.append(ss)
        else:
            lambda_init = 0.8 - 0.6 * math.exp(-0.3 * i)
            w_out = diff_w_out
            qkv_p, *rows_p = _diff_in(hp, diff_w_in, rows_p, heads=diff_heads, n_layers=n_diff,
                                      layer=j, name=f"diff_in_p{i}")
            qkv_s, *rows_s = _diff_in(hs, diff_w_in, rows_s, heads=diff_heads, n_layers=n_diff,
                                      layer=j, name=f"diff_in_s{i}")
            mp = _diff_attention_prompt(qkv_p, tiles_p, bias_far, diff_lambda[j], diff_subln_g[j],
                                        plan, batch=B, heads=diff_heads, lambda_init=lambda_init,
                                        name=f"diff_attn_p{i}")
            ms = _diff_attention_sample(qkv_s, cache_k, cache_v, j, bias_cache, bias_new,
                                        diff_lambda[j], diff_subln_g[j], lambda_init=lambda_init,
                                        name=f"diff_attn_s{i}")
        xp, fp = _proj_norm_res(mp, w_out, j, xp, g[1], g[2], name=f"mix_out_p{i}")
        xs, fs = _proj_norm_res(ms, w_out, j, xs, g[1], g[2], name=f"mix_out_s{i}")
        if i + 1 < depth and (i + 1) % N_MIXERS != 0:
            g_next = norm_g[i + 1, 0]
            xp, hp = _ffn(xp, fp, ffn_w_gu, ffn_w_down, i, g[3], g_next, name=f"ffn_p{i}")
            xs, hs = _ffn(xs, fs, ffn_w_gu, ffn_w_down, i, g[3], g_next, name=f"ffn_s{i}")
        else:
            xp = _ffn(xp, fp, ffn_w_gu, ffn_w_down, i, g[3], name=f"ffn_p{i}")
            xs = _ffn(xs, fs, ffn_w_gu, ffn_w_down, i, g[3], name=f"ffn_s{i}")

    dvh = D // diff_heads
    return (xp.reshape(B, S, D), xs.reshape(DB, n_s, D),
            jnp.stack(ret_p), jnp.stack(ret_s),
            rows_p[0].reshape(n_diff, B, S, diff_heads, dvh),
            rows_p[1].reshape(n_diff, B, S, diff_heads, dvh),
            rows_s[0].reshape(n_diff, DB, n_s, diff_heads, dvh),
            rows_s[1].reshape(n_diff, DB, n_s, diff_heads, dvh))
```
